```python
import jax, jax.numpy as jnp
from jax import lax
import numpy as np

D_MODEL = 1024
BATCH = 32
SEQ = 256
DEPTH = 1
DEC_BATCH = 8
DEC_SEQ = 1024
PAST_LEN = 256

GRID_W = 64
EPS = 1e-6
FOURIER_GROUPS = 8
FOURIER_GROUP_DIM = 128
FOURIER_DIM = FOURIER_GROUPS * FOURIER_GROUP_DIM
D_INNER = 2 * D_MODEL
HEAD_DIM = 64
N_HEADS = D_INNER // HEAD_DIM
N_GROUPS = 4
HEADS_PER_GROUP = N_HEADS // N_GROUPS
D_STATE = 128
GN = N_GROUPS * D_STATE
CONV_DIM = D_INNER + 2 * GN
CONV_W = 5
CHUNK = 128
COL_Z = FOURIER_DIM
COL_XBC = COL_Z + D_INNER
COL_DT = COL_XBC + CONV_DIM
COL_GATE = COL_DT + 2 * N_HEADS
D_IN_PROJ = COL_GATE + 2 * D_MODEL
PEER_HEADS = 8
N_KEYS = 128
N_EXPERTS = N_KEYS * N_KEYS
D_KEY = 256
HALF_KEY = D_KEY // 2
TOPK = 16
TOKEN_BLOCK = 128

kernel_name = 'hybrid_fnet_ssd_peer_diffusion_step'


def _rmsnorm(x, w):
    xf = x.astype(jnp.float32)
    y = xf * lax.rsqrt(jnp.mean(xf * xf, axis=-1, keepdims=True) + EPS)
    return (y * w.astype(jnp.float32)).astype(x.dtype)


def _grid_posemb(rows, dtype):
    r = jnp.repeat(jnp.arange(rows, dtype=jnp.float32), GRID_W)
    col = jnp.tile(jnp.arange(GRID_W, dtype=jnp.float32), rows)
    quarter = D_MODEL // 4
    omega = 1.0 / (10000.0 ** (jnp.arange(quarter, dtype=jnp.float32) / quarter))
    ar = r[:, None] * omega
    ac = col[:, None] * omega
    return jnp.concatenate([jnp.sin(ar), jnp.cos(ar), jnp.sin(ac), jnp.cos(ac)], axis=-1).astype(dtype)


def _fourier_mix(u):
    b, L, _ = u.shape
    ug = u.astype(jnp.float32).reshape(b, L, FOURIER_GROUPS, FOURIER_GROUP_DIM)
    f = jnp.fft.fft2(ug, axes=(1, 3), norm='ortho').real
    return f.reshape(b, L, FOURIER_DIM).astype(u.dtype)


def _centred_dwconv(u, w, bias):
    pad = CONV_W // 2
    L = u.shape[1]
    up = jnp.pad(u, ((0, 0), (pad, pad), (0, 0)))
    out = bias
    for k in range(CONV_W):
        out = out + up[:, k:k + L] * w[k]
    return out


def _ssd_scan(x, dt_raw, Bm, Cm, a_log, dt_bias, d_skip, h0):
    f32 = jnp.float32
    b, L = x.shape[:2]
    nc = L // CHUNK
    dt = jax.nn.softplus(dt_raw.astype(f32) + dt_bias.astype(f32))
    dA = dt * (-jnp.exp(a_log.astype(f32)))
    xf = x.astype(f32)
    xc = (xf * dt[..., None]).reshape(b, nc, CHUNK, N_GROUPS, HEADS_PER_GROUP, HEAD_DIM)
    dAc = dA.reshape(b, nc, CHUNK, N_GROUPS, HEADS_PER_GROUP)
    Bc = Bm.astype(f32).reshape(b, nc, CHUNK, N_GROUPS, D_STATE)
    Cc = Cm.astype(f32).reshape(b, nc, CHUNK, N_GROUPS, D_STATE)
    cs = jnp.cumsum(dAc, axis=2)
    lower = jnp.tril(jnp.ones((CHUNK, CHUNK), dtype=bool))[:, :, None, None]
    seg = cs[:, :, :, None] - cs[:, :, None, :]
    decay_in = jnp.exp(jnp.where(lower, seg, -jnp.inf))
    cb = jnp.einsum('bclgn,bcsgn->bclsg', Cc, Bc)
    y_diag = jnp.einsum('bclsg,bclsgr,bcsgrp->bclgrp', cb, decay_in, xc)
    decay_to_end = jnp.exp(cs[:, :, -1:] - cs)
    chunk_states = jnp.einsum('bctgn,bctgr,bctgrp->bcgrpn', Bc, decay_to_end, xc)
    chunk_decay = jnp.exp(cs[:, :, -1])

    def carry_state(h, inp):
        st, dec = inp
        return h * dec[..., None, None] + st, h

    h_init = h0.astype(f32).reshape(b, N_GROUPS, HEADS_PER_GROUP, HEAD_DIM, D_STATE)
    h_last, h_enter = lax.scan(carry_state, h_init,
                               (jnp.moveaxis(chunk_states, 1, 0), jnp.moveaxis(chunk_decay, 1, 0)))
    h_enter = jnp.moveaxis(h_enter, 0, 1)
    y_off = jnp.einsum('bctgn,bcgrpn,bctgr->bctgrp', Cc, h_enter, jnp.exp(cs))
    y = (y_diag + y_off).reshape(b, L, N_HEADS, HEAD_DIM) + d_skip.astype(f32)[:, None] * xf
    return y, h_last.reshape(b, N_HEADS, HEAD_DIM, D_STATE)


def _ssd_branch(u_z, u_xbc, u_dt, w_conv, b_conv, a_log, dt_bias, d_skip, w_norm, h0):
    b, L = u_xbc.shape[:2]
    xbc = jax.nn.silu(_centred_dwconv(u_xbc, w_conv, b_conv))
    x = xbc[..., :D_INNER].reshape(b, L, N_HEADS, HEAD_DIM)
    Bm = xbc[..., D_INNER:D_INNER + GN].reshape(b, L, N_GROUPS, D_STATE)
    Cm = xbc[..., D_INNER + GN:].reshape(b, L, N_GROUPS, D_STATE)
    rev = lambda t: jnp.flip(t, axis=1)
    y_f, h_f = _ssd_scan(x, u_dt[..., :N_HEADS], Bm, Cm, a_log[0], dt_bias[0], d_skip[0], h0[:, 0])
    y_b, h_b = _ssd_scan(rev(x), rev(u_dt[..., N_HEADS:]), rev(Bm), rev(Cm),
                         a_log[1], dt_bias[1], d_skip[1], h0[:, 1])
    y = (y_f + rev(y_b)).reshape(b, L, D_INNER)
    y = _rmsnorm(y * jax.nn.silu(u_z.astype(jnp.float32)), w_norm).astype(u_z.dtype)
    return y, jnp.stack([h_f, h_b], axis=1)


def _token_mixer(h, h0, w_in, w_conv, b_conv, a_log, dt_bias, d_skip, ssd_norm, w_fourier, w_ssd_out, w_out):
    b, L, _ = h.shape
    proj = h @ w_in
    a_out = _fourier_mix(proj[..., :FOURIER_DIM]) @ w_fourier
    s_y, h_fin = _ssd_branch(proj[..., COL_Z:COL_XBC], proj[..., COL_XBC:COL_DT], proj[..., COL_DT:COL_GATE],
                             w_conv, b_conv, a_log, dt_bias, d_skip, ssd_norm, h0)
    b_out = s_y @ w_ssd_out
    gates = jax.nn.sigmoid(proj[..., COL_GATE:].astype(jnp.float32)).reshape(b, L, 2, D_MODEL)
    merged = (gates[:, :, 0] * a_out.astype(jnp.float32) + gates[:, :, 1] * b_out.astype(jnp.float32)).astype(h.dtype)
    return merged @ w_out, h_fin


def _peer(h, w_query, sub_keys, expert_u, expert_v):
    shape = h.shape
    blocks = h.reshape(-1, TOKEN_BLOCK, D_MODEL)

    def block_fn(xb):
        t = xb.shape[0]
        q = (xb @ w_query).reshape(t, PEER_HEADS, 2, HALF_KEY)
        s = jnp.einsum('thik,hink->thin', q, sub_keys).astype(jnp.float32)
        v_half, i_half = lax.top_k(s, TOPK)
        cand = (v_half[:, :, 0, :, None] + v_half[:, :, 1, None, :]).reshape(t, PEER_HEADS, TOPK * TOPK)
        cand_idx = (i_half[:, :, 0, :, None] * N_KEYS + i_half[:, :, 1, None, :]).reshape(t, PEER_HEADS, TOPK * TOPK)
        top_v, top_pos = lax.top_k(cand, TOPK)
        idx = jnp.take_along_axis(cand_idx, top_pos, axis=-1)
        g = jax.nn.softmax(top_v, axis=-1)
        u = expert_u[idx]
        act = jax.nn.gelu(jnp.einsum('thkd,td->thk', u, xb).astype(jnp.float32))
        v = expert_v[idx]
        return jnp.einsum('thk,thkd->td', (g * act).astype(xb.dtype), v)

    return lax.map(block_fn, blocks).reshape(shape)


def _layer(x, cond, h0, lw):
    (w_ada, b_ada, n_mix_pre, n_mix_post, n_ffn_pre, n_ffn_post, w_in, w_conv, b_conv, a_log, dt_bias,
     d_skip, ssd_norm, w_fourier, w_ssd_out, w_out, peer_query, peer_sub_keys, peer_u, peer_v) = lw
    m = jax.nn.silu(cond.astype(jnp.float32)) @ w_ada.astype(jnp.float32) + b_ada.astype(jnp.float32)
    m = m.astype(x.dtype)[:, None, :]
    shift1, scale1, gate1, shift2, scale2, gate2 = jnp.split(m, 6, axis=-1)
    h = _rmsnorm(x, n_mix_pre) * (1 + scale1) + shift1
    mix, h_fin = _token_mixer(h, h0, w_in, w_conv, b_conv, a_log, dt_bias, d_skip, ssd_norm,
                              w_fourier, w_ssd_out, w_out)
    x = x + gate1 * _rmsnorm(mix, n_mix_post)
    h = _rmsnorm(x, n_ffn_pre) * (1 + scale2) + shift2
    x = x + gate2 * _rmsnorm(_peer(h, peer_query, peer_sub_keys, peer_u, peer_v), n_ffn_post)
    return x, h_fin


def setup_inputs(seed: int = 0) -> dict:
    key = jax.random.key(seed)
    ks = jax.random.split(key, 32)
    f32 = jnp.float32
    nrm = lambda k, shape, s: jax.random.normal(k, shape, f32) * s
    dt0 = jnp.exp(jax.random.uniform(ks[14], (DEPTH, 2, N_HEADS), f32, np.log(1e-3), np.log(1e-1)))
    return {
        'x_prompt': nrm(ks[0], (BATCH, SEQ, D_MODEL), 1.0),
        'x_sample': nrm(ks[1], (DEC_BATCH, DEC_SEQ, D_MODEL), 1.0),
        'state_ssd': nrm(ks[2], (DEC_BATCH, DEPTH, 2, N_HEADS, HEAD_DIM, D_STATE), 0.1),
        'c': nrm(ks[3], (DEC_BATCH, D_MODEL), 1.0),
        'c_ctx': nrm(ks[4], (D_MODEL,), 1.0),
        'w_ada': nrm(ks[5], (DEPTH, D_MODEL, 6 * D_MODEL), 0.3 * D_MODEL ** -0.5),
        'b_ada': nrm(ks[6], (DEPTH, 6 * D_MODEL), 0.02),
        'norm_mix_pre': 1.0 + nrm(ks[7], (DEPTH, D_MODEL), 0.02),
        'norm_mix_post': 1.0 + nrm(ks[8], (DEPTH, D_MODEL), 0.02),
        'norm_ffn_pre': 1.0 + nrm(ks[9], (DEPTH, D_MODEL), 0.02),
        'norm_ffn_post': 1.0 + nrm(ks[10], (DEPTH, D_MODEL), 0.02),
        'w_in': nrm(ks[11], (DEPTH, D_MODEL, D_IN_PROJ), D_MODEL ** -0.5),
        'w_conv': nrm(ks[12], (DEPTH, CONV_W, CONV_DIM), CONV_W ** -0.5),
        'b_conv': nrm(ks[13], (DEPTH, CONV_DIM), 0.02),
        'a_log': jnp.log(jax.random.uniform(ks[15], (DEPTH, 2, N_HEADS), f32, 1.0, 16.0)),
        'dt_bias': dt0 + jnp.log(-jnp.expm1(-dt0)),
        'd_skip': 1.0 + nrm(ks[16], (DEPTH, 2, N_HEADS), 0.1),
        'ssd_norm': 1.0 + nrm(ks[17], (DEPTH, D_INNER), 0.02),
        'w_fourier': nrm(ks[18], (DEPTH, FOURIER_DIM, D_MODEL), FOURIER_DIM ** -0.5),
        'w_ssd_out': nrm(ks[19], (DEPTH, D_INNER, D_MODEL), D_INNER ** -0.5),
        'w_out': nrm(ks[20], (DEPTH, D_MODEL, D_MODEL), D_MODEL ** -0.5),
        'peer_query': nrm(ks[21], (DEPTH, D_MODEL, PEER_HEADS * D_KEY), D_MODEL ** -0.5),
        'peer_sub_keys': nrm(ks[22], (DEPTH, PEER_HEADS, 2, N_KEYS, HALF_KEY), HALF_KEY ** -0.5),
        'peer_u': nrm(ks[23], (DEPTH, N_EXPERTS, D_MODEL), D_MODEL ** -0.5),
        'peer_v': nrm(ks[24], (DEPTH, N_EXPERTS, D_MODEL), D_MODEL ** -0.5),
    }


def reference(x_prompt, x_sample, state_ssd, c, c_ctx, w_ada, b_ada, norm_mix_pre, norm_mix_post,
              norm_ffn_pre, norm_ffn_post, w_in, w_conv, b_conv, a_log, dt_bias, d_skip, ssd_norm,
              w_fourier, w_ssd_out, w_out, peer_query, peer_sub_keys, peer_u, peer_v):
    rows = x_sample.shape[1] // GRID_W
    ctx = x_prompt
    lat = x_sample + _grid_posemb(rows, x_sample.dtype)[None]
    zero_state = jnp.zeros((x_prompt.shape[0], 2, N_HEADS, HEAD_DIM, D_STATE), jnp.float32)
    ctx_states = []
    for l in range(DEPTH):
        lw = (w_ada[l], b_ada[l], norm_mix_pre[l], norm_mix_post[l], norm_ffn_pre[l], norm_ffn_post[l],
              w_in[l], w_conv[l], b_conv[l], a_log[l], dt_bias[l], d_skip[l], ssd_norm[l],
              w_fourier[l], w_ssd_out[l], w_out[l], peer_query[l], peer_sub_keys[l], peer_u[l], peer_v[l])
        ctx, h_ctx = _layer(ctx, c_ctx[None, :], zero_state, lw)
        ctx_states.append(h_ctx.astype(x_prompt.dtype))
        lat, _ = _layer(lat, c, state_ssd[:, l], lw)
    y_prompt = ctx
    y_sample = lat
    new_state_ssd = jnp.stack(ctx_states, axis=1)
    return (y_prompt, y_sample, new_state_ssd)
```

```python
import functools
import math

import numpy as np
import jax
import jax.numpy as jnp
from jax import lax
from jax.experimental import pallas as pl
from jax.experimental.pallas import tpu as pltpu

F32 = jnp.float32
BF16 = jnp.bfloat16

EPS = 1e-6
GRID_W = 64
FOURIER_GROUPS = 8
FOURIER_GROUP_DIM = 128
HEAD_DIM = 64
N_GROUPS = 4
D_STATE = 128
CONV_W = 5
CHUNK = 128
PEER_HEADS = 8
N_KEYS = 128
TOPK = 16

V7X_VMEM_BYTES = 64 * 1024 * 1024
VMEM_LIMIT_BYTES = V7X_VMEM_BYTES - 8 * 1024 * 1024
LANES = 128

_NT = (((1,), (1,)), ((), ()))


def _params(*sem):
    return pltpu.CompilerParams(dimension_semantics=sem, vmem_limit_bytes=VMEM_LIMIT_BYTES)


def _pick(n, candidates):
    for c in candidates:
        if n % c == 0:
            return c
    raise ValueError(f"no tile in {candidates} divides {n}")


def _dot(a, b):
    return jnp.dot(a, b, preferred_element_type=F32)


def _dot_nt(a, b):
    return lax.dot_general(a, b, _NT, preferred_element_type=F32)


def _split3(x):
    x1 = x.astype(BF16)
    r1 = x - x1.astype(F32)
    x2 = r1.astype(BF16)
    x3 = (r1 - x2.astype(F32)).astype(BF16)
    return x1, x2, x3


def _sigmoid(x):
    return 1.0 / (1.0 + jnp.exp(-x))


def _silu(x):
    return x * _sigmoid(x)


def _rms(x, w):
    return x * lax.rsqrt(jnp.mean(x * x, axis=-1, keepdims=True) + EPS) * w


def _mods_kernel(c_ref, w_ref, b_ref, o_ref):
    s = _silu(c_ref[...])
    s1, s2, s3 = _split3(s)
    w1, w2, w3 = _split3(w_ref[...])
    acc = _dot(s1, w1) + (_dot(s1, w2) + _dot(s2, w1)) + (_dot(s1, w3) + _dot(s2, w2) + _dot(s3, w1))
    o_ref[...] = acc + b_ref[...]


def _mods(cond, w_ada, b_ada):
    rows, d = cond.shape
    n = w_ada.shape[1]
    tn = _pick(n, (1024, 512, 256, 128))
    return pl.pallas_call(
        _mods_kernel,
        out_shape=jax.ShapeDtypeStruct((rows, n), F32),
        grid=(n // tn,),
        in_specs=[pl.BlockSpec((rows, d), lambda j: (0, 0)),
                  pl.BlockSpec((d, tn), lambda j: (0, j)),
                  pl.BlockSpec((1, tn), lambda j: (0, j))],
        out_specs=pl.BlockSpec((rows, tn), lambda j: (0, j)),
        compiler_params=_params("arbitrary"),
        name="mods",
    )(cond, w_ada, b_ada.reshape(1, n))


def _norm1_kernel(xc_ref, xs_ref, pos_ref, mod_ref, w_ref, x0_ref, hh_ref, hl_ref, *, n_ctx_tiles):
    i = pl.program_id(0)
    x = jnp.where(i < n_ctx_tiles, xc_ref[...], xs_ref[...] + pos_ref[...])
    x0_ref[...] = x
    h = _rms(x, w_ref[...]) * (1.0 + mod_ref[0, 1:2, :]) + mod_ref[0, 0:1, :]
    hh = h.astype(BF16)
    hh_ref[...] = hh
    hl_ref[...] = (h - hh.astype(F32)).astype(BF16)


def _norm1(xc, xs, pos, mods3, w, *, l_lat, n_lat):
    n_ctx, d = xc.shape
    n_tok = n_ctx + xs.shape[0]
    tm = 256
    nct = n_ctx // tm
    tpl = l_lat // tm
    return pl.pallas_call(
        functools.partial(_norm1_kernel, n_ctx_tiles=nct),
        out_shape=(jax.ShapeDtypeStruct((n_tok, d), F32),
                   jax.ShapeDtypeStruct((n_tok, d), BF16),
                   jax.ShapeDtypeStruct((n_tok, d), BF16)),
        grid=(n_tok // tm,),
        in_specs=[pl.BlockSpec((tm, d), lambda i: (jnp.minimum(i, nct - 1), 0)),
                  pl.BlockSpec((tm, d), lambda i: (jnp.maximum(i - nct, 0), 0)),
                  pl.BlockSpec((tm, d), lambda i: (lax.rem(jnp.maximum(i - nct, 0), tpl), 0)),
                  pl.BlockSpec((1, 6, d), lambda i: (jnp.where(i < nct, n_lat, lax.div(jnp.maximum(i - nct, 0), tpl)), 0, 0)),
                  pl.BlockSpec((1, d), lambda i: (0, 0))],
        out_specs=(pl.BlockSpec((tm, d), lambda i: (i, 0)),
                   pl.BlockSpec((tm, d), lambda i: (i, 0)),
                   pl.BlockSpec((tm, d), lambda i: (i, 0))),
        compiler_params=_params("arbitrary"),
        name="norm1",
    )(xc, xs, pos, mods3, w.reshape(1, d))


def _mm_kernel(a_ref, b_ref, o_ref):
    o_ref[...] = _dot(a_ref[...], b_ref[...]).astype(o_ref.dtype)


def _mm(a, b, out_dtype):
    m, k = a.shape
    n = b.shape[1]
    tm = _pick(m, (1024, 512, 256))
    tn = _pick(n, (1024, 512, 256, 128))
    return pl.pallas_call(
        _mm_kernel,
        out_shape=jax.ShapeDtypeStruct((m, n), out_dtype),
        grid=(m // tm, n // tn),
        in_specs=[pl.BlockSpec((tm, k), lambda i, j: (i, 0)),
                  pl.BlockSpec((k, tn), lambda i, j: (0, j))],
        out_specs=pl.BlockSpec((tm, tn), lambda i, j: (i, j)),
        compiler_params=_params("arbitrary", "arbitrary"),
        name="mm",
    )(a, b)


def _dt_kernel(hh_ref, hl_ref, wh_ref, wl_ref, o_ref):
    hh = hh_ref[...]
    wh = wh_ref[...]
    o_ref[...] = _dot(hh, wh) + (_dot(hh, wl_ref[...]) + _dot(hl_ref[...], wh))


def _dt_proj(hh, hl, wh, wl):
    m, k = hh.shape
    n = wh.shape[1]
    tm = _pick(m, (1024, 512, 256))
    return pl.pallas_call(
        _dt_kernel,
        out_shape=jax.ShapeDtypeStruct((m, n), F32),
        grid=(m // tm,),
        in_specs=[pl.BlockSpec((tm, k), lambda i: (i, 0)),
                  pl.BlockSpec((tm, k), lambda i: (i, 0)),
                  pl.BlockSpec((k, n), lambda i: (0, 0)),
                  pl.BlockSpec((k, n), lambda i: (0, 0))],
        out_specs=pl.BlockSpec((tm, n), lambda i: (i, 0)),
        compiler_params=_params("arbitrary"),
        name="dt_proj",
    )(hh, hl, wh, wl)


def _dft_mats(n):
    idx = np.arange(n, dtype=np.int64)
    ang = 2.0 * np.pi * ((idx[:, None] * idx[None, :]) % n).astype(np.float64) / n
    scale = 1.0 / math.sqrt(n)
    return np.cos(ang) * scale, np.sin(ang) * scale


def _fourier_kernel(u_ref, cs_ref, cl_ref, sl_ref, wf_ref, o_ref, csb_ref, clb_ref, slb_ref):
    @pl.when(pl.program_id(0) == 0)
    def _():
        csb_ref[...] = cs_ref[...].astype(BF16)
        clb_ref[...] = cl_ref[...].astype(BF16)
        slb_ref[...] = sl_ref[...].astype(BF16)

    u = u_ref[...]
    pc, ps = [], []
    for g in range(FOURIER_GROUPS):
        p = _dot(u[:, g * FOURIER_GROUP_DIM:(g + 1) * FOURIER_GROUP_DIM], csb_ref[...])
        pc.append(p[:, :FOURIER_GROUP_DIM])
        ps.append(p[:, FOURIER_GROUP_DIM:])
    pc = jnp.concatenate(pc, axis=1).astype(BF16)
    ps = jnp.concatenate(ps, axis=1).astype(BF16)
    f = _dot(clb_ref[...], pc) - _dot(slb_ref[...], ps)
    o_ref[...] = _dot(f.astype(BF16), wf_ref[...]).astype(o_ref.dtype)


def _fourier(proj, w_fourier, *, seq_len, n_seq, row0, col_block):
    fd = FOURIER_GROUPS * FOURIER_GROUP_DIM
    d = w_fourier.shape[1]
    cc, sc = _dft_mats(FOURIER_GROUP_DIM)
    cs = np.concatenate([cc, sc], axis=1)
    cl, sl = _dft_mats(seq_len)
    rb0 = row0 // seq_len
    return pl.pallas_call(
        _fourier_kernel,
        out_shape=jax.ShapeDtypeStruct((n_seq * seq_len, d), BF16),
        grid=(n_seq,),
        in_specs=[pl.BlockSpec((seq_len, fd), lambda s: (rb0 + s, col_block)),
                  pl.BlockSpec((FOURIER_GROUP_DIM, 2 * FOURIER_GROUP_DIM), lambda s: (0, 0)),
                  pl.BlockSpec((seq_len, seq_len), lambda s: (0, 0)),
                  pl.BlockSpec((seq_len, seq_len), lambda s: (0, 0)),
                  pl.BlockSpec((fd, d), lambda s: (0, 0))],
        out_specs=pl.BlockSpec((seq_len, d), lambda s: (s, 0)),
        scratch_shapes=[pltpu.VMEM(cs.shape, BF16), pltpu.VMEM(cl.shape, BF16), pltpu.VMEM(sl.shape, BF16)],
        compiler_params=_params("arbitrary"),
        name="fourier",
    )(proj, jnp.asarray(cs, F32), jnp.asarray(cl, F32), jnp.asarray(sl, F32), w_fourier)


def _conv_kernel(u_ref, w_ref, b_ref, o_ref):
    u = u_ref[...].astype(F32)
    n = u.shape[0]
    row = lax.broadcasted_iota(jnp.int32, u.shape, 0)
    half = CONV_W // 2
    acc = b_ref[...] + u * w_ref[half:half + 1, :]
    for k in range(CONV_W):
        off = k - half
        if off == 0:
            continue
        shifted = pltpu.roll(u, (-off) % n, axis=0)
        valid = jnp.logical_and(row + off >= 0, row + off < n)
        acc = acc + jnp.where(valid, shifted, 0.0) * w_ref[k:k + 1, :]
    o_ref[...] = _silu(acc).astype(o_ref.dtype)


def _conv(proj, w_conv, b_conv, *, seq_len, n_seq, row0, col0):
    cdim = w_conv.shape[1]
    tc = 512
    rb0 = row0 // seq_len
    cb0 = col0 // tc
    return pl.pallas_call(
        _conv_kernel,
        out_shape=jax.ShapeDtypeStruct((n_seq * seq_len, cdim), BF16),
        grid=(n_seq, cdim // tc),
        in_specs=[pl.BlockSpec((seq_len, tc), lambda s, j: (rb0 + s, cb0 + j)),
                  pl.BlockSpec((CONV_W, tc), lambda s, j: (0, j)),
                  pl.BlockSpec((1, tc), lambda s, j: (0, j))],
        out_specs=pl.BlockSpec((seq_len, tc), lambda s, j: (s, j)),
        compiler_params=_params("arbitrary", "arbitrary"),
        name="conv",
    )(proj, w_conv, b_conv.reshape(1, cdim))


HEADS_PER_GROUP = 8


def _ssd_kernel(*refs, has_h0, want_hfin, n_chunks):
    it = iter(refs)
    fwd = [next(it) for _ in range(4)]
    bwd = [next(it) for _ in range(4)]
    par_ref = next(it)
    dsk_ref = next(it)
    h0_ref = next(it) if has_h0 else None
    yf_ref, yb_ref = next(it), next(it)
    hfin_ref = next(it) if want_hfin else None
    st_ref = next(it)

    c = pl.program_id(2)

    @pl.when(c == 0)
    def _():
        if has_h0:
            st_ref[...] = h0_ref[0]
        else:
            st_ref[...] = jnp.zeros(st_ref.shape, F32)

    li = lax.broadcasted_iota(jnp.int32, (CHUNK, CHUNK), 0)
    si = lax.broadcasted_iota(jnp.int32, (CHUNK, CHUNK), 1)
    a_row = -jnp.exp(par_ref[0, 0:1, :])
    bias_row = par_ref[0, 1:2, :]

    for d in range(2):
        x_ref, b_ref, c_ref, dt_ref = fwd if d == 0 else bwd
        y_ref = yf_ref if d == 0 else yb_ref
        mask = (si <= li) if d == 0 else (si >= li)
        raw = dt_ref[...] + bias_row
        dt = jnp.maximum(raw, 0.0) + jnp.log1p(jnp.exp(-jnp.abs(raw)))
        da = dt * a_row
        d1, d2, d3 = _split3(da)
        mb = mask.astype(BF16)
        cs = _dot(mb, d1) + _dot(mb, d2) + _dot(mb, d3)
        tot = cs[CHUNK - 1:CHUNK, :] if d == 0 else cs[0:1, :]
        w_t = (dt * jnp.exp(tot - cs)).T
        cs_t = cs.T
        dt_t = dt.T
        ecs = jnp.exp(cs)
        etot = jnp.exp(tot)
        bm = b_ref[...]
        cm = c_ref[...]
        cb = _dot_nt(cm, bm)
        x = x_ref[...]
        x_t = x.astype(F32).T
        for k in range(HEADS_PER_GROUP):
            lane = d * HEADS_PER_GROUP + k
            hs = slice(k * HEAD_DIM, (k + 1) * HEAD_DIM)
            seg = cs[:, lane:lane + 1] - cs_t[lane:lane + 1, :]
            dec = jnp.exp(jnp.where(mask, seg, -jnp.inf))
            lm = (cb * dec * dt_t[lane:lane + 1, :]).astype(BF16)
            xh = x[:, hs]
            hprev = st_ref[d, k]
            y = _dot(lm, xh)
            y = y + _dot_nt(cm, hprev.astype(BF16)) * ecs[:, lane:lane + 1]
            y = y + dsk_ref[d:d + 1, hs] * xh.astype(F32)
            y_ref[:, hs] = y
            xw = (x_t[hs, :] * w_t[lane:lane + 1, :]).astype(BF16)
            st_ref[d, k] = hprev * etot[:, lane:lane + 1] + _dot(xw, bm)

    if want_hfin:
        @pl.when(c == n_chunks - 1)
        def _():
            hfin_ref[0] = st_ref[...]


def _ssd(xbc, dt, par, dsk, h0, *, seq_len, n_seq, dt_row0, want_hfin):
    n_tok = xbc.shape[0]
    dt_cb0 = dt_row0 // CHUNK
    d_inner = N_GROUPS * HEADS_PER_GROUP * HEAD_DIM
    nc = seq_len // CHUNK
    gw = HEADS_PER_GROUP * HEAD_DIM
    b_cb = d_inner // D_STATE
    c_cb = b_cb + N_GROUPS
    n_heads = N_GROUPS * HEADS_PER_GROUP

    def rows(direction):
        if direction == 0:
            return lambda s, g, c: s * nc + c
        return lambda s, g, c: s * nc + (nc - 1 - c)

    in_specs = []
    for direction in range(2):
        r = rows(direction)
        in_specs += [pl.BlockSpec((CHUNK, gw), lambda s, g, c, r=r: (r(s, g, c), g)),
                     pl.BlockSpec((CHUNK, D_STATE), lambda s, g, c, r=r: (r(s, g, c), b_cb + g)),
                     pl.BlockSpec((CHUNK, D_STATE), lambda s, g, c, r=r: (r(s, g, c), c_cb + g)),
                     pl.BlockSpec((CHUNK, LANES), lambda s, g, c, r=r: (dt_cb0 + r(s, g, c), g))]
    in_specs += [pl.BlockSpec((1, 8, LANES), lambda s, g, c: (g, 0, 0)),
                 pl.BlockSpec((2, gw), lambda s, g, c: (0, g))]
    args = [xbc, xbc, xbc, dt, xbc, xbc, xbc, dt, par, dsk]
    st_block = (1, 2, HEADS_PER_GROUP, HEAD_DIM, D_STATE)
    if h0 is not None:
        in_specs.append(pl.BlockSpec(st_block, lambda s, g, c: (s, 0, g, 0, 0)))
        args.append(h0)
    rf, rb = rows(0), rows(1)
    out_shape = [jax.ShapeDtypeStruct((n_tok, d_inner), F32), jax.ShapeDtypeStruct((n_tok, d_inner), F32)]
    out_specs = [pl.BlockSpec((CHUNK, gw), lambda s, g, c: (rf(s, g, c), g)),
                 pl.BlockSpec((CHUNK, gw), lambda s, g, c: (rb(s, g, c), g))]
    if want_hfin:
        out_shape.append(jax.ShapeDtypeStruct((n_seq, 2, n_heads, HEAD_DIM, D_STATE), F32))
        out_specs.append(pl.BlockSpec(st_block, lambda s, g, c: (s, 0, g, 0, 0)))
    return pl.pallas_call(
        functools.partial(_ssd_kernel, has_h0=h0 is not None, want_hfin=want_hfin, n_chunks=nc),
        out_shape=tuple(out_shape),
        grid=(n_seq, N_GROUPS, nc),
        in_specs=in_specs,
        out_specs=tuple(out_specs),
        scratch_shapes=[pltpu.VMEM((2, HEADS_PER_GROUP, HEAD_DIM, D_STATE), F32)],
        compiler_params=_params("arbitrary", "arbitrary", "arbitrary"),
        name="ssd",
    )(*args)


def _merge_kernel(yf_ref, yb_ref, z_ref, g_ref, a_ref, x0_ref, mod_ref, nssd_ref, wso_ref, wo_ref,
                  npost_ref, nffn_ref, x1_ref, h2_ref):
    d = x0_ref.shape[1]
    z = z_ref[...].astype(F32)
    y = (yf_ref[...] + yb_ref[...]) * _silu(z)
    y = _rms(y, nssd_ref[...])
    b_out = _dot(y.astype(BF16), wso_ref[...])
    g = g_ref[...].astype(F32)
    merged = _sigmoid(g[:, :d]) * a_ref[...].astype(F32) + _sigmoid(g[:, d:]) * b_out
    mix = _dot(merged.astype(BF16), wo_ref[...])
    x1 = x0_ref[...] + mod_ref[0, 2:3, :] * _rms(mix, npost_ref[...])
    x1_ref[...] = x1
    h2 = _rms(x1, nffn_ref[...]) * (1.0 + mod_ref[0, 4:5, :]) + mod_ref[0, 3:4, :]
    h2_ref[...] = h2.astype(BF16)


def _mod_index(tile, n_ctx_tiles, tiles_per_lat, n_lat):
    t = jnp.maximum(tile - n_ctx_tiles, 0)
    return jnp.where(tile < n_ctx_tiles, n_lat, lax.div(t, tiles_per_lat))


def _merge(yf, yb, proj, a_out, x0, mods3, ssd_norm, wso, wo, npost, nffn, *, z_cb, g_cb, n_ctx, l_lat, n_lat):
    n_tok, d = x0.shape
    d_inner = yf.shape[1]
    tm = 256
    nct, tpl = n_ctx // tm, l_lat // tm
    row = lambda i: (i, 0)
    const = lambda i: (0, 0)
    return pl.pallas_call(
        _merge_kernel,
        out_shape=(jax.ShapeDtypeStruct((n_tok, d), F32), jax.ShapeDtypeStruct((n_tok, d), BF16)),
        grid=(n_tok // tm,),
        in_specs=[pl.BlockSpec((tm, d_inner), row),
                  pl.BlockSpec((tm, d_inner), row),
                  pl.BlockSpec((tm, d_inner), lambda i: (i, z_cb)),
                  pl.BlockSpec((tm, 2 * d), lambda i: (i, g_cb)),
                  pl.BlockSpec((tm, d), row),
                  pl.BlockSpec((tm, d), row),
                  pl.BlockSpec((1, 6, d), lambda i: (_mod_index(i, nct, tpl, n_lat), 0, 0)),
                  pl.BlockSpec((1, d_inner), const),
                  pl.BlockSpec((d_inner, d), const),
                  pl.BlockSpec((d, d), const),
                  pl.BlockSpec((1, d), const),
                  pl.BlockSpec((1, d), const)],
        out_specs=(pl.BlockSpec((tm, d), row), pl.BlockSpec((tm, d), row)),
        compiler_params=_params("arbitrary"),
        name="merge",
    )(yf, yb, proj, proj, a_out, x0, mods3, ssd_norm.reshape(1, d_inner), wso, wo,
      npost.reshape(1, d), nffn.reshape(1, d))


def _top_values(v, count, out_ref=None):
    work = v
    tops = []
    for r in range(count):
        mx = jnp.max(work, axis=0, keepdims=True)
        tops.append(mx)
        if out_ref is not None:
            out_ref[r:r + 1, :] = mx
        if r + 1 < count:
            work = jnp.where(work == mx, -1.0, work)
    return tops


def _router_kernel(h2_ref, wq_ref, keys_ref, e1_ref, e2_ref, th_ref, a_ref, b_ref, cand_ref, candn_ref):
    q = _dot(h2_ref[...], wq_ref[...])
    for h in range(PEER_HEADS):
        es = []
        for half, top_ref in enumerate((a_ref, b_ref)):
            c0 = (h * 2 + half) * N_KEYS
            s_t = _dot_nt(keys_ref[h, half], q[:, c0:c0 + N_KEYS].astype(BF16))
            e = jnp.exp(s_t - jnp.max(s_t, axis=0, keepdims=True))
            es.append(e)
            _top_values(e, TOPK, top_ref)
        b = b_ref[...]
        for k in range(TOPK):
            cand_ref[k * TOPK:(k + 1) * TOPK, :] = a_ref[k:k + 1, :] * b
        cand = cand_ref[...]
        best = _top_values(cand, TOPK)
        theta = best[-1]
        inv_z = 1.0 / functools.reduce(lambda u, v: u + v, best)
        for k in range(TOPK):
            candn_ref[k * TOPK:(k + 1) * TOPK, :] = b * (a_ref[k:k + 1, :] * inv_z)
        th_ref[h:h + 1, :] = jnp.min(jnp.where(cand >= theta, candn_ref[...], jnp.inf), axis=0, keepdims=True)
        e1_ref[h] = es[0] * inv_z
        e2_ref[h] = es[1]


def _router(h2, wq, keys):
    n_tok, d = h2.shape
    tb = 256
    return pl.pallas_call(
        _router_kernel,
        out_shape=(jax.ShapeDtypeStruct((PEER_HEADS, N_KEYS, n_tok), F32),
                   jax.ShapeDtypeStruct((PEER_HEADS, N_KEYS, n_tok), F32),
                   jax.ShapeDtypeStruct((PEER_HEADS, n_tok), F32)),
        grid=(n_tok // tb,),
        in_specs=[pl.BlockSpec((tb, d), lambda i: (i, 0)),
                  pl.BlockSpec(wq.shape, lambda i: (0, 0)),
                  pl.BlockSpec(keys.shape, lambda i: (0, 0, 0, 0))],
        out_specs=(pl.BlockSpec((PEER_HEADS, N_KEYS, tb), lambda i: (0, 0, i)),
                   pl.BlockSpec((PEER_HEADS, N_KEYS, tb), lambda i: (0, 0, i)),
                   pl.BlockSpec((PEER_HEADS, tb), lambda i: (0, i))),
        scratch_shapes=[pltpu.VMEM((TOPK, tb), F32), pltpu.VMEM((TOPK, tb), F32),
                        pltpu.VMEM((TOPK * TOPK, tb), F32), pltpu.VMEM((TOPK * TOPK, tb), F32)],
        compiler_params=_params("arbitrary"),
        name="router",
    )(h2, wq, keys)


def _gelu_tanh(x):
    return 0.5 * x * (1.0 + jnp.tanh(math.sqrt(2.0 / math.pi) * (x + 0.044715 * (x * x * x))))


def _peer_kernel(h2_ref, u_ref, vt_ref, e1_ref, e2_ref, th_ref, x1_ref, mod_ref, npost_ref, y_ref,
                 acc_ref, at_ref, gw_ref, *, n_ec):
    ec = pl.program_id(1)
    e_chunk, tb = at_ref.shape
    n_i = e_chunk // N_KEYS

    @pl.when(ec == 0)
    def _():
        acc_ref[...] = jnp.zeros(acc_ref.shape, F32)

    at_ref[...] = _dot_nt(u_ref[...], h2_ref[...])
    for ii in range(n_i):
        rs = slice(ii * N_KEYS, (ii + 1) * N_KEYS)
        for lb in range(tb // LANES):
            ls = slice(lb * LANES, (lb + 1) * LANES)
            w = jnp.zeros((N_KEYS, LANES), F32)
            for h in range(PEER_HEADS):
                p = e2_ref[h, :, ls] * e1_ref[h, ii:ii + 1, ls]
                w = w + jnp.where(p >= th_ref[h:h + 1, ls], p, 0.0)
            gw_ref[rs, ls] = (w * _gelu_tanh(at_ref[rs, ls])).astype(BF16)
    acc_ref[...] += _dot(vt_ref[...], gw_ref[...])

    @pl.when(ec == n_ec - 1)
    def _():
        o = acc_ref[...].T
        y_ref[...] = x1_ref[...] + mod_ref[0, 5:6, :] * _rms(o, npost_ref[...])


def _peer(h2, u, vt, e1, e2, th, x1, mods3, npost, *, n_ctx, l_lat, n_lat):
    n_tok, d = h2.shape
    n_exp = u.shape[0]
    tb = 512
    e_chunk = 1024
    n_ec = n_exp // e_chunk
    nct, tpl = n_ctx // tb, l_lat // tb
    return pl.pallas_call(
        functools.partial(_peer_kernel, n_ec=n_ec),
        out_shape=jax.ShapeDtypeStruct((n_tok, d), F32),
        grid=(n_tok // tb, n_ec),
        in_specs=[pl.BlockSpec((tb, d), lambda i, e: (i, 0)),
                  pl.BlockSpec((e_chunk, d), lambda i, e: (e, 0)),
                  pl.BlockSpec((d, e_chunk), lambda i, e: (0, e)),
                  pl.BlockSpec((PEER_HEADS, e_chunk // N_KEYS, tb), lambda i, e: (0, e, i)),
                  pl.BlockSpec((PEER_HEADS, N_KEYS, tb), lambda i, e: (0, 0, i)),
                  pl.BlockSpec((PEER_HEADS, tb), lambda i, e: (0, i)),
                  pl.BlockSpec((tb, d), lambda i, e: (i, 0)),
                  pl.BlockSpec((1, 6, d), lambda i, e: (_mod_index(i, nct, tpl, n_lat), 0, 0)),
                  pl.BlockSpec((1, d), lambda i, e: (0, 0))],
        out_specs=pl.BlockSpec((tb, d), lambda i, e: (i, 0)),
        scratch_shapes=[pltpu.VMEM((d, tb), F32),
                        pltpu.VMEM((e_chunk, tb), F32),
                        pltpu.VMEM((e_chunk, tb), BF16)],
        compiler_params=_params("arbitrary", "arbitrary"),
        name="peer",
    )(h2, u, vt, e1, e2, th, x1, mods3, npost.reshape(1, d))


def _grid_posemb(rows, d):
    r = np.repeat(np.arange(rows, dtype=np.float32), GRID_W)
    col = np.tile(np.arange(GRID_W, dtype=np.float32), rows)
    quarter = d // 4
    omega = (1.0 / (10000.0 ** (np.arange(quarter, dtype=np.float32) / np.float32(quarter)))).astype(np.float32)
    ar = r[:, None] * omega
    ac = col[:, None] * omega
    return np.concatenate([np.sin(ar), np.cos(ar), np.sin(ac), np.cos(ac)], axis=-1).astype(np.float32)


def kernel(x_prompt, x_sample, state_ssd, c, c_ctx, w_ada, b_ada, norm_mix_pre, norm_mix_post, norm_ffn_pre,
           norm_ffn_post, w_in, w_conv, b_conv, a_log, dt_bias, d_skip, ssd_norm, w_fourier, w_ssd_out, w_out,
           peer_query, peer_sub_keys, peer_u, peer_v):
    n_ctx_seq, l_ctx, d = x_prompt.shape
    n_lat, l_lat, _ = x_sample.shape
    depth = w_ada.shape[0]
    assert depth == 1
    n_ctx = n_ctx_seq * l_ctx
    assert n_ctx % l_lat == 0 and l_ctx % 256 == 0 and l_lat % 512 == 0 and n_ctx % 512 == 0
    fd = FOURIER_GROUPS * FOURIER_GROUP_DIM
    d_inner = 2 * d
    n_heads = d_inner // HEAD_DIM
    gn = N_GROUPS * D_STATE
    conv_dim = d_inner + 2 * gn
    col_z, col_xbc, col_dt = fd, fd + d_inner, fd + d_inner + conv_dim
    col_gate = col_dt + 2 * n_heads
    l = 0

    rows = n_lat + 1
    rows_pad = -(-rows // 8) * 8
    cond = jnp.concatenate([c, c_ctx[None, :], jnp.zeros((rows_pad - rows, d), F32)], axis=0)
    mods3 = _mods(cond, w_ada[l], b_ada[l]).reshape(rows_pad, 6, d)

    pos = jnp.asarray(_grid_posemb(l_lat // GRID_W, d))
    x0, hh, hl = _norm1(x_prompt.reshape(n_ctx, d), x_sample.reshape(n_lat * l_lat, d), pos, mods3,
                        norm_mix_pre[l], l_lat=l_lat, n_lat=n_lat)

    wi = w_in[l]
    w_main = jnp.concatenate([wi[:, col_z:col_xbc], wi[:, col_gate:], wi[:, :fd], wi[:, col_xbc:col_dt]],
                             axis=1).astype(BF16)
    proj = _mm(hh, w_main, BF16)
    z_cb, g_cb = 0, 1
    f_cb = (2 * d_inner) // fd
    xbc_col0 = 2 * d_inner + fd

    wdt = wi[:, col_dt:col_gate].reshape(d, 2, N_GROUPS, HEADS_PER_GROUP)
    wdt = jnp.transpose(wdt, (0, 2, 1, 3)).reshape(d, N_GROUPS, 2 * HEADS_PER_GROUP)
    wdt = jnp.pad(wdt, ((0, 0), (0, 0), (0, LANES - 2 * HEADS_PER_GROUP))).reshape(d, N_GROUPS * LANES)
    wdt_h = wdt.astype(BF16)
    wdt_l = (wdt - wdt_h.astype(F32)).astype(BF16)
    dt_raw = _dt_proj(hh, hl, wdt_h, wdt_l)

    def lanes16(p):
        p = jnp.transpose(p.reshape(2, N_GROUPS, HEADS_PER_GROUP), (1, 0, 2)).reshape(N_GROUPS, 2 * HEADS_PER_GROUP)
        return jnp.pad(p, ((0, 0), (0, LANES - 2 * HEADS_PER_GROUP)))

    par = jnp.stack([lanes16(a_log[l]), lanes16(dt_bias[l])] + [jnp.zeros((N_GROUPS, LANES), F32)] * 6, axis=1)
    dsk = jnp.repeat(d_skip[l], HEAD_DIM, axis=1)

    wf = w_fourier[l].astype(BF16)
    segs = ((l_ctx, n_ctx_seq, 0, None, True), (l_lat, n_lat, n_ctx, state_ssd[:, l], False))
    a_parts, yf_parts, yb_parts = [], [], []
    hfin = None
    for seq_len, n_seq, row0, h0, want_hfin in segs:
        a_parts.append(_fourier(proj, wf, seq_len=seq_len, n_seq=n_seq, row0=row0, col_block=f_cb))
        xbc = _conv(proj, w_conv[l], b_conv[l], seq_len=seq_len, n_seq=n_seq, row0=row0, col0=xbc_col0)
        res = _ssd(xbc, dt_raw, par, dsk, h0, seq_len=seq_len, n_seq=n_seq, dt_row0=row0, want_hfin=want_hfin)
        yf_parts.append(res[0])
        yb_parts.append(res[1])
        if want_hfin:
            hfin = res[2]
    a_out = jnp.concatenate(a_parts, axis=0)
    yf = jnp.concatenate(yf_parts, axis=0)
    yb = jnp.concatenate(yb_parts, axis=0)

    x1, h2 = _merge(yf, yb, proj, a_out, x0, mods3, ssd_norm[l], w_ssd_out[l].astype(BF16), w_out[l].astype(BF16),
                    norm_mix_post[l], norm_ffn_pre[l], z_cb=z_cb, g_cb=g_cb, n_ctx=n_ctx, l_lat=l_lat, n_lat=n_lat)

    e1, e2, th = _router(h2, peer_query[l].astype(BF16), peer_sub_keys[l].astype(BF16))
    y = _peer(h2, peer_u[l].astype(BF16), peer_v[l].T.astype(BF16), e1, e2, th, x1, mods3, norm_ffn_post[l],
              n_ctx=n_ctx, l_lat=l_lat, n_lat=n_lat)

    y_prompt = y[:n_ctx].reshape(n_ctx_seq, l_ctx, d)
    y_sample = y[n_ctx:].reshape(n_lat, l_lat, d)
    new_state = hfin.reshape(n_ctx_seq, 1, 2, n_heads, HEAD_DIM, D_STATE).astype(x_prompt.dtype)
    return (y_prompt, y_sample, new_state)
```

```python
import functools
import math

import numpy as np
import jax
import jax.numpy as jnp
from jax import lax
from jax.experimental import pallas as pl
from jax.experimental.pallas import tpu as pltpu

F32 = jnp.float32
BF16 = jnp.bfloat16

EPS = 1e-6
GRID_W = 64
FOURIER_GROUPS = 8
FOURIER_GROUP_DIM = 128
HEAD_DIM = 64
N_GROUPS = 4
HEADS_PER_GROUP = 8
D_STATE = 128
CONV_W = 5
CHUNK = 128
PEER_HEADS = 8
N_KEYS = 128
TOPK = 16

V7X_VMEM_BYTES = 64 * 1024 * 1024
VMEM_LIMIT_BYTES = V7X_VMEM_BYTES - 8 * 1024 * 1024
LANES = 128

_NT = (((1,), (1,)), ((), ()))


def _params(*sem, flags=None):
    return pltpu.CompilerParams(dimension_semantics=sem, vmem_limit_bytes=VMEM_LIMIT_BYTES, flags=flags)


def _pick(n, candidates):
    for c in candidates:
        if n % c == 0:
            return c
    raise ValueError(f"no tile in {candidates} divides {n}")


def _dot(a, b):
    return jnp.dot(a, b, preferred_element_type=F32)


def _dot_nt(a, b):
    return lax.dot_general(a, b, _NT, preferred_element_type=F32)


def _split3(x):
    x1 = x.astype(BF16)
    r1 = x - x1.astype(F32)
    x2 = r1.astype(BF16)
    x3 = (r1 - x2.astype(F32)).astype(BF16)
    return x1, x2, x3


def _sigmoid(x):
    return 1.0 / (1.0 + jnp.exp(-x))


def _silu(x):
    return x * _sigmoid(x)


def _rms(x, w):
    return x * lax.rsqrt(jnp.mean(x * x, axis=-1, keepdims=True) + EPS) * w


def _mods_kernel(c_ref, w_ref, b_ref, o_ref):
    s = _silu(c_ref[...])
    s1, s2, s3 = _split3(s)
    w1, w2, w3 = _split3(w_ref[...])
    acc = _dot(s1, w1) + (_dot(s1, w2) + _dot(s2, w1)) + (_dot(s1, w3) + _dot(s2, w2) + _dot(s3, w1))
    o_ref[...] = acc + b_ref[...]


def _mods(cond, w_ada, b_ada):
    rows, d = cond.shape
    n = w_ada.shape[1]
    tn = _pick(n, (1024, 512, 256, 128))
    return pl.pallas_call(
        _mods_kernel,
        out_shape=jax.ShapeDtypeStruct((rows, n), F32),
        grid=(n // tn,),
        in_specs=[pl.BlockSpec((rows, d), lambda j: (0, 0)),
                  pl.BlockSpec((d, tn), lambda j: (0, j)),
                  pl.BlockSpec((1, tn), lambda j: (0, j))],
        out_specs=pl.BlockSpec((rows, tn), lambda j: (0, j)),
        compiler_params=_params("arbitrary"),
        name="mods",
    )(cond, w_ada, b_ada.reshape(1, n))


def _mod_index(tile, n_ctx_tiles, tiles_per_lat, n_lat):
    t = jnp.maximum(tile - n_ctx_tiles, 0)
    return jnp.where(tile < n_ctx_tiles, n_lat, lax.div(t, tiles_per_lat))


def _norm1_kernel(xc_ref, xs_ref, pos_ref, mod_ref, w_ref, x0_ref, hh_ref, hl_ref, *, n_ctx_tiles):
    i = pl.program_id(0)
    x = jnp.where(i < n_ctx_tiles, xc_ref[...], xs_ref[...] + pos_ref[...])
    x0_ref[...] = x
    h = _rms(x, w_ref[...]) * (1.0 + mod_ref[0, 1:2, :]) + mod_ref[0, 0:1, :]
    hh = h.astype(BF16)
    hh_ref[...] = hh
    hl_ref[...] = (h - hh.astype(F32)).astype(BF16)


def _norm1(xc, xs, pos, mods3, w, *, l_lat, n_lat):
    n_ctx, d = xc.shape
    n_tok = n_ctx + xs.shape[0]
    tm = 256
    nct = n_ctx // tm
    tpl = l_lat // tm
    return pl.pallas_call(
        functools.partial(_norm1_kernel, n_ctx_tiles=nct),
        out_shape=(jax.ShapeDtypeStruct((n_tok, d), F32),
                   jax.ShapeDtypeStruct((n_tok, d), BF16),
                   jax.ShapeDtypeStruct((n_tok, d), BF16)),
        grid=(n_tok // tm,),
        in_specs=[pl.BlockSpec((tm, d), lambda i: (jnp.minimum(i, nct - 1), 0)),
                  pl.BlockSpec((tm, d), lambda i: (jnp.maximum(i - nct, 0), 0)),
                  pl.BlockSpec((tm, d), lambda i: (lax.rem(jnp.maximum(i - nct, 0), tpl), 0)),
                  pl.BlockSpec((1, 6, d), lambda i: (_mod_index(i, nct, tpl, n_lat), 0, 0)),
                  pl.BlockSpec((1, d), lambda i: (0, 0))],
        out_specs=(pl.BlockSpec((tm, d), lambda i: (i, 0)),
                   pl.BlockSpec((tm, d), lambda i: (i, 0)),
                   pl.BlockSpec((tm, d), lambda i: (i, 0))),
        compiler_params=_params("arbitrary"),
        name="norm1",
    )(xc, xs, pos, mods3, w.reshape(1, d))


def _mm_kernel(a_ref, b_ref, o_ref):
    o_ref[...] = _dot(a_ref[...], b_ref[...]).astype(o_ref.dtype)


def _mm(a, b, out_dtype):
    m, k = a.shape
    n = b.shape[1]
    tm = _pick(m, (1024, 512, 256))
    tn = _pick(n, (1024, 512, 256, 128))
    return pl.pallas_call(
        _mm_kernel,
        out_shape=jax.ShapeDtypeStruct((m, n), out_dtype),
        grid=(m // tm, n // tn),
        in_specs=[pl.BlockSpec((tm, k), lambda i, j: (i, 0)),
                  pl.BlockSpec((k, tn), lambda i, j: (0, j))],
        out_specs=pl.BlockSpec((tm, tn), lambda i, j: (i, j)),
        compiler_params=_params("arbitrary", "arbitrary"),
        name="mm",
    )(a, b)


def _dt_kernel(hh_ref, hl_ref, wh_ref, wl_ref, o_ref):
    hh = hh_ref[...]
    wh = wh_ref[...]
    o_ref[...] = _dot(hh, wh) + (_dot(hh, wl_ref[...]) + _dot(hl_ref[...], wh))


def _dt_proj(hh, hl, wh, wl):
    m, k = hh.shape
    n = wh.shape[1]
    tm = _pick(m, (1024, 512, 256))
    return pl.pallas_call(
        _dt_kernel,
        out_shape=jax.ShapeDtypeStruct((m, n), F32),
        grid=(m // tm,),
        in_specs=[pl.BlockSpec((tm, k), lambda i: (i, 0)),
                  pl.BlockSpec((tm, k), lambda i: (i, 0)),
                  pl.BlockSpec((k, n), lambda i: (0, 0)),
                  pl.BlockSpec((k, n), lambda i: (0, 0))],
        out_specs=pl.BlockSpec((tm, n), lambda i: (i, 0)),
        compiler_params=_params("arbitrary"),
        name="dt_proj",
    )(hh, hl, wh, wl)


def _dft_mats(n):
    idx = np.arange(n, dtype=np.int64)
    ang = 2.0 * np.pi * ((idx[:, None] * idx[None, :]) % n).astype(np.float64) / n
    scale = 1.0 / math.sqrt(n)
    return np.cos(ang) * scale, np.sin(ang) * scale


def _span_dft(span, seq_len):
    c, s = _dft_mats(seq_len)
    eye = np.eye(span // seq_len)
    return np.kron(eye, c), np.kron(eye, s)


def _fourier_kernel(u_ref, cs_ref, cl_ref, sl_ref, wf_ref, o_ref, csb_ref, clb_ref, slb_ref, *, n_ctx_spans):
    s = pl.program_id(0)

    @pl.when(jnp.logical_or(s == 0, s == n_ctx_spans))
    def _():
        csb_ref[...] = cs_ref[...].astype(BF16)
        clb_ref[...] = cl_ref[0].astype(BF16)
        slb_ref[...] = sl_ref[0].astype(BF16)

    u = u_ref[...]
    pc, ps = [], []
    for g in range(FOURIER_GROUPS):
        p = _dot(u[:, g * FOURIER_GROUP_DIM:(g + 1) * FOURIER_GROUP_DIM], csb_ref[...])
        pc.append(p[:, :FOURIER_GROUP_DIM])
        ps.append(p[:, FOURIER_GROUP_DIM:])
    pc = jnp.concatenate(pc, axis=1).astype(BF16)
    ps = jnp.concatenate(ps, axis=1).astype(BF16)
    f = _dot(clb_ref[...], pc) - _dot(slb_ref[...], ps)
    o_ref[...] = _dot(f.astype(BF16), wf_ref[...]).astype(o_ref.dtype)


def _fourier(proj, w_fourier, *, span, l_ctx, n_ctx_spans, col_block):
    n_tok = proj.shape[0]
    fd = FOURIER_GROUPS * FOURIER_GROUP_DIM
    d = w_fourier.shape[1]
    cc, sc = _dft_mats(FOURIER_GROUP_DIM)
    cs = np.concatenate([cc, sc], axis=1)
    c_ctx, s_ctx = _span_dft(span, l_ctx)
    c_lat, s_lat = _dft_mats(span)
    cl = jnp.asarray(np.stack([c_ctx, c_lat]), F32)
    sl = jnp.asarray(np.stack([s_ctx, s_lat]), F32)
    kind = lambda s: (jnp.where(s < n_ctx_spans, 0, 1), 0, 0)
    return pl.pallas_call(
        functools.partial(_fourier_kernel, n_ctx_spans=n_ctx_spans),
        out_shape=jax.ShapeDtypeStruct((n_tok, d), BF16),
        grid=(n_tok // span,),
        in_specs=[pl.BlockSpec((span, fd), lambda s: (s, col_block)),
                  pl.BlockSpec(cs.shape, lambda s: (0, 0)),
                  pl.BlockSpec((1, span, span), kind),
                  pl.BlockSpec((1, span, span), kind),
                  pl.BlockSpec((fd, d), lambda s: (0, 0))],
        out_specs=pl.BlockSpec((span, d), lambda s: (s, 0)),
        scratch_shapes=[pltpu.VMEM(cs.shape, BF16), pltpu.VMEM((span, span), BF16), pltpu.VMEM((span, span), BF16)],
        compiler_params=_params("arbitrary"),
        name="fourier",
    )(proj, jnp.asarray(cs, F32), cl, sl, w_fourier)


def _conv_kernel(u_ref, w_ref, b_ref, o_ref, *, n_ctx_spans, l_ctx):
    u = u_ref[...].astype(F32)
    n = u.shape[0]
    seq_len = jnp.where(pl.program_id(0) < n_ctx_spans, l_ctx, n)
    pos = jnp.bitwise_and(lax.broadcasted_iota(jnp.int32, u.shape, 0), seq_len - 1)
    half = CONV_W // 2
    acc = b_ref[...] + u * w_ref[half:half + 1, :]
    for k in range(CONV_W):
        off = k - half
        if off == 0:
            continue
        shifted = pltpu.roll(u, (-off) % n, axis=0)
        valid = jnp.logical_and(pos + off >= 0, pos + off < seq_len)
        acc = acc + jnp.where(valid, shifted, 0.0) * w_ref[k:k + 1, :]
    o_ref[...] = _silu(acc).astype(o_ref.dtype)


def _conv(proj, w_conv, b_conv, *, span, l_ctx, n_ctx_spans, col0):
    n_tok = proj.shape[0]
    cdim = w_conv.shape[1]
    tc = 512
    cb0 = col0 // tc
    return pl.pallas_call(
        functools.partial(_conv_kernel, n_ctx_spans=n_ctx_spans, l_ctx=l_ctx),
        out_shape=jax.ShapeDtypeStruct((n_tok, cdim), BF16),
        grid=(n_tok // span, cdim // tc),
        in_specs=[pl.BlockSpec((span, tc), lambda s, j: (s, cb0 + j)),
                  pl.BlockSpec((CONV_W, tc), lambda s, j: (0, j)),
                  pl.BlockSpec((1, tc), lambda s, j: (0, j))],
        out_specs=pl.BlockSpec((span, tc), lambda s, j: (s, j)),
        compiler_params=_params("arbitrary", "arbitrary"),
        name="conv",
    )(proj, w_conv, b_conv.reshape(1, cdim))


PAIRS_PER_GROUP = HEADS_PER_GROUP // 2


def _ssd_kernel(xf_ref, bf_ref, cf_ref, dtf_ref, xb_ref, bb_ref, cb_ref, dtb_ref, par_ref, dsk_ref, h0_ref,
                yf_ref, yb_ref, hff_ref, hfb_ref, st_ref, *, n_ctx_spans, chunks_per_ctx_seq):
    s = pl.program_id(0)
    c = pl.program_id(2)
    is_ctx = s < n_ctx_spans
    cps = chunks_per_ctx_seq
    seq_start = jnp.where(is_ctx, lax.rem(c, cps) == 0, c == 0)
    seq_end = jnp.logical_and(is_ctx, lax.rem(c, cps) == cps - 1)

    @pl.when(jnp.logical_and(seq_start, is_ctx))
    def _():
        st_ref[...] = jnp.zeros(st_ref.shape, F32)

    @pl.when(jnp.logical_and(seq_start, jnp.logical_not(is_ctx)))
    def _():
        st_ref[...] = h0_ref[0]

    li = lax.broadcasted_iota(jnp.int32, (CHUNK, CHUNK), 0)
    si = lax.broadcasted_iota(jnp.int32, (CHUNK, CHUNK), 1)
    lane_lo = si < HEAD_DIM
    row_lo = li < HEAD_DIM
    a_row = -jnp.exp(par_ref[0, 0:1, :])
    bias_row = par_ref[0, 1:2, :]

    for d, (x_ref, b_ref, c_ref, dt_ref, y_ref) in enumerate(((xf_ref, bf_ref, cf_ref, dtf_ref, yf_ref),
                                                                 (xb_ref, bb_ref, cb_ref, dtb_ref, yb_ref))):
        mask = (si <= li) if d == 0 else (si >= li)
        raw = dt_ref[...] + bias_row
        dt = jnp.maximum(raw, 0.0) + jnp.log1p(jnp.exp(-jnp.abs(raw)))
        da = dt * a_row
        d1, d2, d3 = _split3(da)
        mb = mask.astype(BF16)
        cs = _dot(mb, d1) + _dot(mb, d2) + _dot(mb, d3)
        tot = cs[CHUNK - 1:CHUNK, :] if d == 0 else cs[0:1, :]
        w_t = (dt * jnp.exp(tot - cs)).T
        cs_t = cs.T
        dt_t = dt.T
        ecs = jnp.exp(cs)
        etot_t = jnp.exp(cs_t[:, CHUNK - 1:CHUNK] if d == 0 else cs_t[:, 0:1])
        bm = b_ref[...]
        cm = c_ref[...]
        cb = _dot_nt(cm, bm)
        x = x_ref[...]
        x_t = x.astype(F32).T
        for kp in range(PAIRS_PER_GROUP):
            l0 = d * HEADS_PER_GROUP + 2 * kp
            l1 = l0 + 1
            ps = slice(kp * LANES, (kp + 1) * LANES)
            xp = x[:, ps]
            lms = []
            for ln in (l0, l1):
                seg = cs[:, ln:ln + 1] - cs_t[ln:ln + 1, :]
                dec = jnp.exp(jnp.where(mask, seg, -jnp.inf))
                lms.append((cb * dec * dt_t[ln:ln + 1, :]).astype(BF16))
            hp = st_ref[d, kp]
            y = jnp.where(lane_lo, _dot(lms[0], xp), _dot(lms[1], xp))
            y = y + _dot_nt(cm, hp.astype(BF16)) * jnp.where(lane_lo, ecs[:, l0:l0 + 1], ecs[:, l1:l1 + 1])
            y = y + dsk_ref[d:d + 1, ps] * xp.astype(F32)
            y_ref[:, ps] = y.astype(y_ref.dtype)
            xw = jnp.where(row_lo, w_t[l0:l0 + 1, :], w_t[l1:l1 + 1, :]) * x_t[ps, :]
            keep = jnp.where(row_lo, etot_t[l0:l0 + 1, :], etot_t[l1:l1 + 1, :])
            st_ref[d, kp] = hp * keep + _dot(xw.astype(BF16), bm)

    @pl.when(seq_end)
    def _():
        hff_ref[0] = st_ref[0]
        hfb_ref[0] = st_ref[1]


def _ssd(xbc, dt, par, dsk, h0, *, span, l_ctx, n_ctx_spans, n_ctx_seq):
    n_tok = xbc.shape[0]
    d_inner = N_GROUPS * HEADS_PER_GROUP * HEAD_DIM
    nc = span // CHUNK
    cps = l_ctx // CHUNK
    spc = span // l_ctx
    gw = HEADS_PER_GROUP * HEAD_DIM
    b_cb = d_inner // D_STATE
    c_cb = b_cb + N_GROUPS
    st_block = (1, PAIRS_PER_GROUP, 2 * HEAD_DIM, D_STATE)

    rf = lambda s, g, c: s * nc + c
    rb = lambda s, g, c: s * nc + (nc - 1 - c)

    def hfin_index(chunk_of):
        def index(s, g, c):
            is_ctx = s < n_ctx_spans
            s_e = jnp.minimum(s, n_ctx_spans - 1)
            g_e = jnp.where(is_ctx, g, N_GROUPS - 1)
            c_e = jnp.where(is_ctx, c, nc - 1)
            return (s_e * spc + lax.div(chunk_of(c_e), cps), g_e, 0, 0)
        return index

    in_specs = []
    for r in (rf, rb):
        in_specs += [pl.BlockSpec((CHUNK, gw), lambda s, g, c, r=r: (r(s, g, c), g)),
                     pl.BlockSpec((CHUNK, D_STATE), lambda s, g, c, r=r: (r(s, g, c), b_cb + g)),
                     pl.BlockSpec((CHUNK, D_STATE), lambda s, g, c, r=r: (r(s, g, c), c_cb + g)),
                     pl.BlockSpec((CHUNK, LANES), lambda s, g, c, r=r: (r(s, g, c), g))]
    in_specs += [pl.BlockSpec((1, 8, LANES), lambda s, g, c: (g, 0, 0)),
                 pl.BlockSpec((2, gw), lambda s, g, c: (0, g)),
                 pl.BlockSpec((1, 2) + st_block[1:], lambda s, g, c: (jnp.maximum(s - n_ctx_spans, 0), 0, g, 0, 0))]
    hf_shape = jax.ShapeDtypeStruct((n_ctx_seq, N_GROUPS * PAIRS_PER_GROUP, 2 * HEAD_DIM, D_STATE), F32)
    return pl.pallas_call(
        functools.partial(_ssd_kernel, n_ctx_spans=n_ctx_spans, chunks_per_ctx_seq=cps),
        out_shape=(jax.ShapeDtypeStruct((n_tok, d_inner), BF16), jax.ShapeDtypeStruct((n_tok, d_inner), BF16),
                   hf_shape, hf_shape),
        grid=(n_tok // span, N_GROUPS, nc),
        in_specs=in_specs,
        out_specs=(pl.BlockSpec((CHUNK, gw), lambda s, g, c: (rf(s, g, c), g)),
                   pl.BlockSpec((CHUNK, gw), lambda s, g, c: (rb(s, g, c), g)),
                   pl.BlockSpec(st_block, hfin_index(lambda c: c)),
                   pl.BlockSpec(st_block, hfin_index(lambda c: nc - 1 - c))),
        scratch_shapes=[pltpu.VMEM((2,) + st_block[1:], F32)],
        compiler_params=_params("arbitrary", "arbitrary", "arbitrary"),
        name="ssd",
    )(xbc, xbc, xbc, dt, xbc, xbc, xbc, dt, par, dsk, h0)


def _merge_kernel(yf_ref, yb_ref, z_ref, g_ref, a_ref, x0_ref, mod_ref, nssd_ref, wso_ref, wo_ref,
                  npost_ref, nffn_ref, x1_ref, h2_ref):
    d = x0_ref.shape[1]
    z = z_ref[...].astype(F32)
    y = (yf_ref[...].astype(F32) + yb_ref[...].astype(F32)) * _silu(z)
    y = _rms(y, nssd_ref[...])
    b_out = _dot(y.astype(BF16), wso_ref[...])
    g = g_ref[...].astype(F32)
    merged = _sigmoid(g[:, :d]) * a_ref[...].astype(F32) + _sigmoid(g[:, d:]) * b_out
    mix = _dot(merged.astype(BF16), wo_ref[...])
    x1 = x0_ref[...] + mod_ref[0, 2:3, :] * _rms(mix, npost_ref[...])
    x1_ref[...] = x1
    h2 = _rms(x1, nffn_ref[...]) * (1.0 + mod_ref[0, 4:5, :]) + mod_ref[0, 3:4, :]
    h2_ref[...] = h2.astype(BF16)


def _merge(yf, yb, proj, a_out, x0, mods3, ssd_norm, wso, wo, npost, nffn, *, z_cb, g_cb, n_ctx, l_lat, n_lat):
    n_tok, d = x0.shape
    d_inner = yf.shape[1]
    tm = 256
    nct, tpl = n_ctx // tm, l_lat // tm
    row = lambda i: (i, 0)
    const = lambda i: (0, 0)
    return pl.pallas_call(
        _merge_kernel,
        out_shape=(jax.ShapeDtypeStruct((n_tok, d), F32), jax.ShapeDtypeStruct((n_tok, d), BF16)),
        grid=(n_tok // tm,),
        in_specs=[pl.BlockSpec((tm, d_inner), row),
                  pl.BlockSpec((tm, d_inner), row),
                  pl.BlockSpec((tm, d_inner), lambda i: (i, z_cb)),
                  pl.BlockSpec((tm, 2 * d), lambda i: (i, g_cb)),
                  pl.BlockSpec((tm, d), row),
                  pl.BlockSpec((tm, d), row),
                  pl.BlockSpec((1, 6, d), lambda i: (_mod_index(i, nct, tpl, n_lat), 0, 0)),
                  pl.BlockSpec((1, d_inner), const),
                  pl.BlockSpec((d_inner, d), const),
                  pl.BlockSpec((d, d), const),
                  pl.BlockSpec((1, d), const),
                  pl.BlockSpec((1, d), const)],
        out_specs=(pl.BlockSpec((tm, d), row), pl.BlockSpec((tm, d), row)),
        compiler_params=_params("arbitrary"),
        name="merge",
    )(yf, yb, proj, proj, a_out, x0, mods3, ssd_norm.reshape(1, d_inner), wso, wo,
      npost.reshape(1, d), nffn.reshape(1, d))


def _top_values(v, count, out_ref=None):
    work = v
    tops = []
    for r in range(count):
        mx = jnp.max(work, axis=0, keepdims=True)
        tops.append(mx)
        if out_ref is not None:
            out_ref[r:r + 1, :] = mx
        if r + 1 < count:
            work = jnp.where(work == mx, -1.0, work)
    return tops


def _router_kernel(h2_ref, wq_ref, keys_ref, e1_ref, e2_ref, th_ref, a_ref, b_ref, cand_ref, candn_ref):
    q = _dot(h2_ref[...], wq_ref[...])
    for h in range(PEER_HEADS):
        es = []
        for half, top_ref in enumerate((a_ref, b_ref)):
            c0 = (h * 2 + half) * N_KEYS
            s_t = _dot_nt(keys_ref[h, half], q[:, c0:c0 + N_KEYS].astype(BF16))
            e = jnp.exp(s_t - jnp.max(s_t, axis=0, keepdims=True))
            es.append(e)
            _top_values(e, TOPK, top_ref)
        b = b_ref[...]
        for k in range(TOPK):
            cand_ref[k * TOPK:(k + 1) * TOPK, :] = a_ref[k:k + 1, :] * b
        cand = cand_ref[...]
        best = _top_values(cand, TOPK)
        theta = best[-1]
        inv_z = 1.0 / functools.reduce(lambda u, v: u + v, best)
        for k in range(TOPK):
            candn_ref[k * TOPK:(k + 1) * TOPK, :] = b * (a_ref[k:k + 1, :] * inv_z)
        th_ref[h:h + 1, :] = jnp.min(jnp.where(cand >= theta, candn_ref[...], jnp.inf), axis=0, keepdims=True)
        e1_ref[h] = es[0] * inv_z
        e2_ref[h] = es[1]


def _router(h2, wq, keys):
    n_tok, d = h2.shape
    tb = 256
    return pl.pallas_call(
        _router_kernel,
        out_shape=(jax.ShapeDtypeStruct((PEER_HEADS, N_KEYS, n_tok), F32),
                   jax.ShapeDtypeStruct((PEER_HEADS, N_KEYS, n_tok), F32),
                   jax.ShapeDtypeStruct((PEER_HEADS, n_tok), F32)),
        grid=(n_tok // tb,),
        in_specs=[pl.BlockSpec((tb, d), lambda i: (i, 0)),
                  pl.BlockSpec(wq.shape, lambda i: (0, 0)),
                  pl.BlockSpec(keys.shape, lambda i: (0, 0, 0, 0))],
        out_specs=(pl.BlockSpec((PEER_HEADS, N_KEYS, tb), lambda i: (0, 0, i)),
                   pl.BlockSpec((PEER_HEADS, N_KEYS, tb), lambda i: (0, 0, i)),
                   pl.BlockSpec((PEER_HEADS, tb), lambda i: (0, i))),
        scratch_shapes=[pltpu.VMEM((TOPK, tb), F32), pltpu.VMEM((TOPK, tb), F32),
                        pltpu.VMEM((TOPK * TOPK, tb), F32), pltpu.VMEM((TOPK * TOPK, tb), F32)],
        compiler_params=_params("arbitrary"),
        name="router",
    )(h2, wq, keys)


PEER_TOKEN_BLOCK = 512
PEER_EXPERT_CHUNK = 512
PEER_CHUNKS_PER_STEP = 4


GELU_C = math.sqrt(2.0 / math.pi)
GELU_K = 0.044715


def _gelu_tanh(x, c, k):
    return 0.5 * x * (1.0 + jnp.tanh(c * (x + k * (x * x * x))))


def _peer_weighted_act(e1_ref, i0, e2_ref, th_ref, gc_ref, at_ref, gw_ref):
    e_chunk, tb = at_ref.shape
    gc = gc_ref[0:1, :].astype(at_ref.dtype)
    gk = gc_ref[1:2, :].astype(at_ref.dtype)
    for ii in range(e_chunk // N_KEYS):
        rs = slice(ii * N_KEYS, (ii + 1) * N_KEYS)
        for lb in range(tb // LANES):
            ls = slice(lb * LANES, (lb + 1) * LANES)
            w = None
            for h in range(PEER_HEADS):
                p = e2_ref[h, :, ls] * e1_ref[h, i0 + ii:i0 + ii + 1, ls]
                sel = jnp.where(p >= th_ref[h:h + 1, ls], p, 0.0)
                w = sel if w is None else w + sel
            gw_ref[rs, ls] = w.astype(BF16) * _gelu_tanh(at_ref[rs, ls], gc, gk)


def _peer_kernel(h2_ref, u_ref, vt_ref, e1_ref, e2_ref, th_ref, gc_ref, x1_ref, mod_ref, npost_ref,
                 yc_ref, ys_ref, acc_ref, *bufs, n_it, n_ctx_blocks):
    it = pl.program_id(1)
    n_sub = PEER_CHUNKS_PER_STEP
    at_refs, gw_refs = bufs[:n_sub], bufs[n_sub:]
    ec = at_refs[0].shape[0]

    @pl.when(it == 0)
    def _():
        acc_ref[...] = jnp.zeros(acc_ref.shape, F32)

    h2 = h2_ref[...]

    def project(k):
        at_refs[k][...] = _dot_nt(u_ref[k * ec:(k + 1) * ec, :], h2).astype(BF16)

    def weigh(k):
        _peer_weighted_act(e1_ref, k * (ec // N_KEYS), e2_ref, th_ref, gc_ref, at_refs[k], gw_refs[k])

    def combine(k):
        acc_ref[...] += _dot(vt_ref[:, k * ec:(k + 1) * ec], gw_refs[k][...])

    project(0)
    for k in range(n_sub):
        if k + 1 < n_sub:
            project(k + 1)
        weigh(k)
        combine(k)

    last = it == n_it - 1
    is_ctx = pl.program_id(0) < n_ctx_blocks

    def result():
        o = acc_ref[...].T
        return x1_ref[...] + mod_ref[0, 5:6, :] * _rms(o, npost_ref[...])

    @pl.when(jnp.logical_and(last, is_ctx))
    def _():
        yc_ref[...] = result()

    @pl.when(jnp.logical_and(last, jnp.logical_not(is_ctx)))
    def _():
        ys_ref[...] = result()


def _peer(h2, u, vt, e1, e2, th, x1, mods3, npost, *, n_ctx, l_lat, n_lat):
    n_tok, d = h2.shape
    n_exp = u.shape[0]
    tb, ec, n_sub = PEER_TOKEN_BLOCK, PEER_EXPERT_CHUNK, PEER_CHUNKS_PER_STEP
    step_e = ec * n_sub
    n_it = n_exp // step_e
    rows_e1 = step_e // N_KEYS
    nct, tpl = n_ctx // tb, l_lat // tb
    gelu_consts = jnp.asarray(np.repeat(np.array([[GELU_C], [GELU_K]] + [[0.0]] * 6, np.float32), LANES, axis=1))
    return pl.pallas_call(
        functools.partial(_peer_kernel, n_it=n_it, n_ctx_blocks=nct),
        out_shape=(jax.ShapeDtypeStruct((n_ctx, d), F32), jax.ShapeDtypeStruct((n_tok - n_ctx, d), F32)),
        grid=(n_tok // tb, n_it),
        in_specs=[pl.BlockSpec((tb, d), lambda i, t: (i, 0)),
                  pl.BlockSpec((step_e, d), lambda i, t: (t, 0)),
                  pl.BlockSpec((d, step_e), lambda i, t: (0, t)),
                  pl.BlockSpec((PEER_HEADS, rows_e1, tb), lambda i, t: (0, t, i)),
                  pl.BlockSpec((PEER_HEADS, N_KEYS, tb), lambda i, t: (0, 0, i)),
                  pl.BlockSpec((PEER_HEADS, tb), lambda i, t: (0, i)),
                  pl.BlockSpec(gelu_consts.shape, lambda i, t: (0, 0)),
                  pl.BlockSpec((tb, d), lambda i, t: (i, 0)),
                  pl.BlockSpec((1, 6, d), lambda i, t: (_mod_index(i, nct, tpl, n_lat), 0, 0)),
                  pl.BlockSpec((1, d), lambda i, t: (0, 0))],
        out_specs=(pl.BlockSpec((tb, d), lambda i, t: (jnp.minimum(i, nct - 1), 0)),
                   pl.BlockSpec((tb, d), lambda i, t: (jnp.maximum(i - nct, 0), 0))),
        scratch_shapes=[pltpu.VMEM((d, tb), F32)] + [pltpu.VMEM((ec, tb), BF16) for _ in range(2 * n_sub)],
        compiler_params=_params("arbitrary", "arbitrary"),
        name="peer",
    )(h2, u, vt, e1, e2, th, gelu_consts, x1, mods3, npost.reshape(1, d))


def _grid_posemb(rows, d):
    r = np.repeat(np.arange(rows, dtype=np.float32), GRID_W)
    col = np.tile(np.arange(GRID_W, dtype=np.float32), rows)
    quarter = d // 4
    omega = (1.0 / (10000.0 ** (np.arange(quarter, dtype=np.float32) / np.float32(quarter)))).astype(np.float32)
    ar = r[:, None] * omega
    ac = col[:, None] * omega
    return np.concatenate([np.sin(ar), np.cos(ar), np.sin(ac), np.cos(ac)], axis=-1).astype(np.float32)


def kernel(x_prompt, x_sample, state_ssd, c, c_ctx, w_ada, b_ada, norm_mix_pre, norm_mix_post, norm_ffn_pre,
           norm_ffn_post, w_in, w_conv, b_conv, a_log, dt_bias, d_skip, ssd_norm, w_fourier, w_ssd_out, w_out,
           peer_query, peer_sub_keys, peer_u, peer_v):
    n_ctx_seq, l_ctx, d = x_prompt.shape
    n_lat, l_lat, _ = x_sample.shape
    depth = w_ada.shape[0]
    assert depth == 1
    n_ctx = n_ctx_seq * l_ctx
    span = l_lat
    assert span % l_ctx == 0 and n_ctx % span == 0 and n_ctx > 0 and n_lat > 0
    assert l_ctx & (l_ctx - 1) == 0 and span & (span - 1) == 0
    assert l_ctx % 256 == 0 and span % PEER_TOKEN_BLOCK == 0 and n_ctx % PEER_TOKEN_BLOCK == 0
    n_ctx_spans = n_ctx // span
    fd = FOURIER_GROUPS * FOURIER_GROUP_DIM
    d_inner = 2 * d
    n_heads = d_inner // HEAD_DIM
    assert n_heads == N_GROUPS * HEADS_PER_GROUP
    gn = N_GROUPS * D_STATE
    conv_dim = d_inner + 2 * gn
    col_z, col_xbc, col_dt = fd, fd + d_inner, fd + d_inner + conv_dim
    col_gate = col_dt + 2 * n_heads
    l = 0

    rows = n_lat + 1
    rows_pad = -(-rows // 8) * 8
    cond = jnp.concatenate([c, c_ctx[None, :], jnp.zeros((rows_pad - rows, d), F32)], axis=0)
    mods3 = _mods(cond, w_ada[l], b_ada[l]).reshape(rows_pad, 6, d)

    pos = jnp.asarray(_grid_posemb(l_lat // GRID_W, d))
    x0, hh, hl = _norm1(x_prompt.reshape(n_ctx, d), x_sample.reshape(n_lat * l_lat, d), pos, mods3,
                        norm_mix_pre[l], l_lat=l_lat, n_lat=n_lat)

    wi = w_in[l]
    w_main = jnp.concatenate([wi[:, col_z:col_xbc], wi[:, col_gate:], wi[:, :fd], wi[:, col_xbc:col_dt]],
                             axis=1).astype(BF16)
    proj = _mm(hh, w_main, BF16)
    z_cb, g_cb = 0, 1
    f_cb = (2 * d_inner) // fd
    xbc_col0 = 2 * d_inner + fd

    wdt = wi[:, col_dt:col_gate].reshape(d, 2, N_GROUPS, HEADS_PER_GROUP)
    wdt = jnp.transpose(wdt, (0, 2, 1, 3)).reshape(d, N_GROUPS, 2 * HEADS_PER_GROUP)
    wdt = jnp.pad(wdt, ((0, 0), (0, 0), (0, LANES - 2 * HEADS_PER_GROUP))).reshape(d, N_GROUPS * LANES)
    wdt_h = wdt.astype(BF16)
    wdt_l = (wdt - wdt_h.astype(F32)).astype(BF16)
    dt_raw = _dt_proj(hh, hl, wdt_h, wdt_l)

    def lanes16(p):
        p = jnp.transpose(p.reshape(2, N_GROUPS, HEADS_PER_GROUP), (1, 0, 2)).reshape(N_GROUPS, 2 * HEADS_PER_GROUP)
        return jnp.pad(p, ((0, 0), (0, LANES - 2 * HEADS_PER_GROUP)))

    par = jnp.stack([lanes16(a_log[l]), lanes16(dt_bias[l])] + [jnp.zeros((N_GROUPS, LANES), F32)] * 6, axis=1)
    dsk = jnp.repeat(d_skip[l], HEAD_DIM, axis=1)

    a_out = _fourier(proj, w_fourier[l].astype(BF16), span=span, l_ctx=l_ctx, n_ctx_spans=n_ctx_spans, col_block=f_cb)
    xbc = _conv(proj, w_conv[l], b_conv[l], span=span, l_ctx=l_ctx, n_ctx_spans=n_ctx_spans, col0=xbc_col0)
    h0 = state_ssd[:, l].reshape(n_lat, 2, N_GROUPS * PAIRS_PER_GROUP, 2 * HEAD_DIM, D_STATE)
    yf, yb, hff, hfb = _ssd(xbc, dt_raw, par, dsk, h0, span=span, l_ctx=l_ctx, n_ctx_spans=n_ctx_spans,
                            n_ctx_seq=n_ctx_seq)

    x1, h2 = _merge(yf, yb, proj, a_out, x0, mods3, ssd_norm[l], w_ssd_out[l].astype(BF16), w_out[l].astype(BF16),
                    norm_mix_post[l], norm_ffn_pre[l], z_cb=z_cb, g_cb=g_cb, n_ctx=n_ctx, l_lat=l_lat, n_lat=n_lat)

    e1, e2, th = _router(h2, peer_query[l].astype(BF16), peer_sub_keys[l].astype(BF16))
    yc, ys = _peer(h2, peer_u[l].astype(BF16), peer_v[l].T.astype(BF16), e1, e2, th, x1, mods3, norm_ffn_post[l],
                   n_ctx=n_ctx, l_lat=l_lat, n_lat=n_lat)

    hfin = jnp.stack([hff, hfb], axis=1)
    new_state = hfin.reshape(n_ctx_seq, 1, 2, n_heads, HEAD_DIM, D_STATE).astype(x_prompt.dtype)
    return (yc.reshape(n_ctx_seq, l_ctx, d), ys.reshape(n_lat, l_lat, d), new_state)
```

```python
import functools
import math

import numpy as np
import jax
import jax.numpy as jnp
from jax import lax
from jax.experimental import pallas as pl
from jax.experimental.pallas import tpu as pltpu

F32 = jnp.float32
BF16 = jnp.bfloat16

EPS = 1e-6
GRID_W = 64
FOURIER_GROUPS = 8
FOURIER_GROUP_DIM = 128
HEAD_DIM = 64
N_GROUPS = 4
HEADS_PER_GROUP = 8
D_STATE = 128
CONV_W = 5
CHUNK = 128
PEER_HEADS = 8
N_KEYS = 128
TOPK = 16

V7X_VMEM_BYTES = 64 * 1024 * 1024
VMEM_LIMIT_BYTES = V7X_VMEM_BYTES - 8 * 1024 * 1024
LANES = 128

_NT = (((1,), (1,)), ((), ()))


def _params(*sem, flags=None):
    return pltpu.CompilerParams(dimension_semantics=sem, vmem_limit_bytes=VMEM_LIMIT_BYTES, flags=flags)


def _pick(n, candidates):
    for c in candidates:
        if n % c == 0:
            return c
    raise ValueError(f"no tile in {candidates} divides {n}")


def _dot(a, b):
    return jnp.dot(a, b, preferred_element_type=F32)


def _dot_nt(a, b):
    return lax.dot_general(a, b, _NT, preferred_element_type=F32)


def _split3(x):
    x1 = x.astype(BF16)
    r1 = x - x1.astype(F32)
    x2 = r1.astype(BF16)
    x3 = (r1 - x2.astype(F32)).astype(BF16)
    return x1, x2, x3


def _sigmoid(x):
    return 1.0 / (1.0 + jnp.exp(-x))


def _silu(x):
    return x * _sigmoid(x)


def _rms(x, w):
    return x * lax.rsqrt(jnp.mean(x * x, axis=-1, keepdims=True) + EPS) * w


def _mods_kernel(c_ref, w_ref, b_ref, o_ref):
    s = _silu(c_ref[...])
    s1, s2, s3 = _split3(s)
    w1, w2, w3 = _split3(w_ref[...])
    acc = _dot(s1, w1) + (_dot(s1, w2) + _dot(s2, w1)) + (_dot(s1, w3) + _dot(s2, w2) + _dot(s3, w1))
    o_ref[...] = acc + b_ref[...]


def _mods(cond, w_ada, b_ada):
    rows, d = cond.shape
    n = w_ada.shape[1]
    tn = _pick(n, (1024, 512, 256, 128))
    return pl.pallas_call(
        _mods_kernel,
        out_shape=jax.ShapeDtypeStruct((rows, n), F32),
        grid=(n // tn,),
        in_specs=[pl.BlockSpec((rows, d), lambda j: (0, 0)),
                  pl.BlockSpec((d, tn), lambda j: (0, j)),
                  pl.BlockSpec((1, tn), lambda j: (0, j))],
        out_specs=pl.BlockSpec((rows, tn), lambda j: (0, j)),
        compiler_params=_params("arbitrary"),
        name="mods",
    )(cond, w_ada, b_ada.reshape(1, n))


def _mod_index(tile, n_ctx_tiles, tiles_per_lat, n_lat):
    t = jnp.maximum(tile - n_ctx_tiles, 0)
    return jnp.where(tile < n_ctx_tiles, n_lat, lax.div(t, tiles_per_lat))


def _norm1_kernel(xc_ref, xs_ref, pos_ref, mod_ref, w_ref, x0_ref, hh_ref, hl_ref, *, n_ctx_tiles):
    i = pl.program_id(0)
    x = jnp.where(i < n_ctx_tiles, xc_ref[...], xs_ref[...] + pos_ref[...])
    x0_ref[...] = x
    h = _rms(x, w_ref[...]) * (1.0 + mod_ref[0, 1:2, :]) + mod_ref[0, 0:1, :]
    hh = h.astype(BF16)
    hh_ref[...] = hh
    hl_ref[...] = (h - hh.astype(F32)).astype(BF16)


def _norm1(xc, xs, pos, mods3, w, *, l_lat, n_lat):
    n_ctx, d = xc.shape
    n_tok = n_ctx + xs.shape[0]
    tm = 256
    nct = n_ctx // tm
    tpl = l_lat // tm
    return pl.pallas_call(
        functools.partial(_norm1_kernel, n_ctx_tiles=nct),
        out_shape=(jax.ShapeDtypeStruct((n_tok, d), F32),
                   jax.ShapeDtypeStruct((n_tok, d), BF16),
                   jax.ShapeDtypeStruct((n_tok, d), BF16)),
        grid=(n_tok // tm,),
        in_specs=[pl.BlockSpec((tm, d), lambda i: (jnp.minimum(i, nct - 1), 0)),
                  pl.BlockSpec((tm, d), lambda i: (jnp.maximum(i - nct, 0), 0)),
                  pl.BlockSpec((tm, d), lambda i: (lax.rem(jnp.maximum(i - nct, 0), tpl), 0)),
                  pl.BlockSpec((1, 6, d), lambda i: (_mod_index(i, nct, tpl, n_lat), 0, 0)),
                  pl.BlockSpec((1, d), lambda i: (0, 0))],
        out_specs=(pl.BlockSpec((tm, d), lambda i: (i, 0)),
                   pl.BlockSpec((tm, d), lambda i: (i, 0)),
                   pl.BlockSpec((tm, d), lambda i: (i, 0))),
        compiler_params=_params("arbitrary"),
        name="norm1",
    )(xc, xs, pos, mods3, w.reshape(1, d))


def _mm_kernel(a_ref, b_ref, o_ref):
    o_ref[...] = _dot(a_ref[...], b_ref[...]).astype(o_ref.dtype)


def _mm(a, b, out_dtype):
    m, k = a.shape
    n = b.shape[1]
    tm = _pick(m, (1024, 512, 256))
    tn = _pick(n, (1024, 512, 256, 128))
    return pl.pallas_call(
        _mm_kernel,
        out_shape=jax.ShapeDtypeStruct((m, n), out_dtype),
        grid=(m // tm, n // tn),
        in_specs=[pl.BlockSpec((tm, k), lambda i, j: (i, 0)),
                  pl.BlockSpec((k, tn), lambda i, j: (0, j))],
        out_specs=pl.BlockSpec((tm, tn), lambda i, j: (i, j)),
        compiler_params=_params("arbitrary", "arbitrary"),
        name="mm",
    )(a, b)


def _dt_kernel(hh_ref, hl_ref, wh_ref, wl_ref, o_ref):
    hh = hh_ref[...]
    wh = wh_ref[...]
    o_ref[...] = _dot(hh, wh) + (_dot(hh, wl_ref[...]) + _dot(hl_ref[...], wh))


def _dt_proj(hh, hl, wh, wl):
    m, k = hh.shape
    n = wh.shape[1]
    tm = _pick(m, (1024, 512, 256))
    return pl.pallas_call(
        _dt_kernel,
        out_shape=jax.ShapeDtypeStruct((m, n), F32),
        grid=(m // tm,),
        in_specs=[pl.BlockSpec((tm, k), lambda i: (i, 0)),
                  pl.BlockSpec((tm, k), lambda i: (i, 0)),
                  pl.BlockSpec((k, n), lambda i: (0, 0)),
                  pl.BlockSpec((k, n), lambda i: (0, 0))],
        out_specs=pl.BlockSpec((tm, n), lambda i: (i, 0)),
        compiler_params=_params("arbitrary"),
        name="dt_proj",
    )(hh, hl, wh, wl)


def _dft_mats(n):
    idx = np.arange(n, dtype=np.int64)
    ang = 2.0 * np.pi * ((idx[:, None] * idx[None, :]) % n).astype(np.float64) / n
    scale = 1.0 / math.sqrt(n)
    return np.cos(ang) * scale, np.sin(ang) * scale


def _span_dft(span, seq_len):
    c, s = _dft_mats(seq_len)
    eye = np.eye(span // seq_len)
    return np.kron(eye, c), np.kron(eye, s)


def _fourier_kernel(u_ref, cs_ref, cl_ref, sl_ref, wf_ref, o_ref, csb_ref, clb_ref, slb_ref, *, n_ctx_spans):
    s = pl.program_id(0)

    @pl.when(jnp.logical_or(s == 0, s == n_ctx_spans))
    def _():
        csb_ref[...] = cs_ref[...].astype(BF16)
        clb_ref[...] = cl_ref[0].astype(BF16)
        slb_ref[...] = sl_ref[0].astype(BF16)

    u = u_ref[...]
    pc, ps = [], []
    for g in range(FOURIER_GROUPS):
        p = _dot(u[:, g * FOURIER_GROUP_DIM:(g + 1) * FOURIER_GROUP_DIM], csb_ref[...])
        pc.append(p[:, :FOURIER_GROUP_DIM])
        ps.append(p[:, FOURIER_GROUP_DIM:])
    pc = jnp.concatenate(pc, axis=1).astype(BF16)
    ps = jnp.concatenate(ps, axis=1).astype(BF16)
    f = _dot(clb_ref[...], pc) - _dot(slb_ref[...], ps)
    o_ref[...] = _dot(f.astype(BF16), wf_ref[...]).astype(o_ref.dtype)


def _fourier(proj, w_fourier, *, span, l_ctx, n_ctx_spans, col_block):
    n_tok = proj.shape[0]
    fd = FOURIER_GROUPS * FOURIER_GROUP_DIM
    d = w_fourier.shape[1]
    cc, sc = _dft_mats(FOURIER_GROUP_DIM)
    cs = np.concatenate([cc, sc], axis=1)
    c_ctx, s_ctx = _span_dft(span, l_ctx)
    c_lat, s_lat = _dft_mats(span)
    cl = jnp.asarray(np.stack([c_ctx, c_lat]), F32)
    sl = jnp.asarray(np.stack([s_ctx, s_lat]), F32)
    kind = lambda s: (jnp.where(s < n_ctx_spans, 0, 1), 0, 0)
    return pl.pallas_call(
        functools.partial(_fourier_kernel, n_ctx_spans=n_ctx_spans),
        out_shape=jax.ShapeDtypeStruct((n_tok, d), BF16),
        grid=(n_tok // span,),
        in_specs=[pl.BlockSpec((span, fd), lambda s: (s, col_block)),
                  pl.BlockSpec(cs.shape, lambda s: (0, 0)),
                  pl.BlockSpec((1, span, span), kind),
                  pl.BlockSpec((1, span, span), kind),
                  pl.BlockSpec((fd, d), lambda s: (0, 0))],
        out_specs=pl.BlockSpec((span, d), lambda s: (s, 0)),
        scratch_shapes=[pltpu.VMEM(cs.shape, BF16), pltpu.VMEM((span, span), BF16), pltpu.VMEM((span, span), BF16)],
        compiler_params=_params("arbitrary"),
        name="fourier",
    )(proj, jnp.asarray(cs, F32), cl, sl, w_fourier)


def _conv_kernel(u_ref, w_ref, b_ref, o_ref, *, n_ctx_spans, l_ctx):
    u = u_ref[...].astype(F32)
    n = u.shape[0]
    seq_len = jnp.where(pl.program_id(0) < n_ctx_spans, l_ctx, n)
    pos = jnp.bitwise_and(lax.broadcasted_iota(jnp.int32, u.shape, 0), seq_len - 1)
    half = CONV_W // 2
    acc = b_ref[...] + u * w_ref[half:half + 1, :]
    for k in range(CONV_W):
        off = k - half
        if off == 0:
            continue
        shifted = pltpu.roll(u, (-off) % n, axis=0)
        valid = jnp.logical_and(pos + off >= 0, pos + off < seq_len)
        acc = acc + jnp.where(valid, shifted, 0.0) * w_ref[k:k + 1, :]
    o_ref[...] = _silu(acc).astype(o_ref.dtype)


def _conv(proj, w_conv, b_conv, *, span, l_ctx, n_ctx_spans, col0):
    n_tok = proj.shape[0]
    cdim = w_conv.shape[1]
    tc = 512
    cb0 = col0 // tc
    return pl.pallas_call(
        functools.partial(_conv_kernel, n_ctx_spans=n_ctx_spans, l_ctx=l_ctx),
        out_shape=jax.ShapeDtypeStruct((n_tok, cdim), BF16),
        grid=(n_tok // span, cdim // tc),
        in_specs=[pl.BlockSpec((span, tc), lambda s, j: (s, cb0 + j)),
                  pl.BlockSpec((CONV_W, tc), lambda s, j: (0, j)),
                  pl.BlockSpec((1, tc), lambda s, j: (0, j))],
        out_specs=pl.BlockSpec((span, tc), lambda s, j: (s, j)),
        compiler_params=_params("arbitrary", "arbitrary"),
        name="conv",
    )(proj, w_conv, b_conv.reshape(1, cdim))


PAIRS_PER_GROUP = HEADS_PER_GROUP // 2


def _ssd_kernel(xf_ref, bf_ref, cf_ref, dtf_ref, xb_ref, bb_ref, cb_ref, dtb_ref, par_ref, dsk_ref, h0_ref,
                yf_ref, yb_ref, hff_ref, hfb_ref, st_ref, *, n_ctx_spans, chunks_per_ctx_seq):
    s = pl.program_id(0)
    c = pl.program_id(2)
    is_ctx = s < n_ctx_spans
    cps = chunks_per_ctx_seq
    seq_start = jnp.where(is_ctx, lax.rem(c, cps) == 0, c == 0)
    seq_end = jnp.logical_and(is_ctx, lax.rem(c, cps) == cps - 1)

    @pl.when(jnp.logical_and(seq_start, is_ctx))
    def _():
        st_ref[...] = jnp.zeros(st_ref.shape, F32)

    @pl.when(jnp.logical_and(seq_start, jnp.logical_not(is_ctx)))
    def _():
        st_ref[...] = h0_ref[0]

    li = lax.broadcasted_iota(jnp.int32, (CHUNK, CHUNK), 0)
    si = lax.broadcasted_iota(jnp.int32, (CHUNK, CHUNK), 1)
    lane_lo = si < HEAD_DIM
    row_lo = li < HEAD_DIM
    gw = HEADS_PER_GROUP * HEAD_DIM
    n_groups = N_GROUPS

    for d, (x_ref, b_ref, c_ref, dt_ref, y_ref) in enumerate(((xf_ref, bf_ref, cf_ref, dtf_ref, yf_ref),
                                                                 (xb_ref, bb_ref, cb_ref, dtb_ref, yb_ref))):
        mask = (si <= li) if d == 0 else (si >= li)
        mb = mask.astype(BF16)
        raw = dt_ref[...] + par_ref[1:2, :]
        dt = jnp.maximum(raw, 0.0) + jnp.log1p(jnp.exp(-jnp.abs(raw)))
        da = dt * (-jnp.exp(par_ref[0:1, :]))
        d1, d2, d3 = _split3(da)
        cs = _dot(mb, d1) + _dot(mb, d2) + _dot(mb, d3)
        tot = cs[CHUNK - 1:CHUNK, :] if d == 0 else cs[0:1, :]
        w_t = (dt * jnp.exp(tot - cs)).T
        cs_t = cs.T
        dt_t = dt.T
        ecs = jnp.exp(cs)
        etot_t = jnp.exp(cs_t[:, CHUNK - 1:CHUNK] if d == 0 else cs_t[:, 0:1])
        for gi in range(n_groups):
            gl = slice(gi * LANES, (gi + 1) * LANES)
            bm = b_ref[:, gl]
            cm = c_ref[:, gl]
            cb = _dot_nt(cm, bm)
            x = x_ref[:, gi * gw:(gi + 1) * gw]
            x_t = x.astype(F32).T
            for kp in range(PAIRS_PER_GROUP):
                l0 = (d * n_groups + gi) * HEADS_PER_GROUP + 2 * kp
                l1 = l0 + 1
                ps = slice(kp * LANES, (kp + 1) * LANES)
                ys = slice(gi * gw + kp * LANES, gi * gw + (kp + 1) * LANES)
                pair = gi * PAIRS_PER_GROUP + kp
                xp = x[:, ps]
                lms = []
                for ln in (l0, l1):
                    seg = cs[:, ln:ln + 1] - cs_t[ln:ln + 1, :]
                    dec = jnp.exp(jnp.where(mask, seg, -jnp.inf))
                    lms.append((cb * dec * dt_t[ln:ln + 1, :]).astype(BF16))
                hp = st_ref[d, pair]
                y = jnp.where(lane_lo, _dot(lms[0], xp), _dot(lms[1], xp))
                y = y + _dot_nt(cm, hp.astype(BF16)) * jnp.where(lane_lo, ecs[:, l0:l0 + 1], ecs[:, l1:l1 + 1])
                y = y + dsk_ref[d:d + 1, ys] * xp.astype(F32)
                y_ref[:, ys] = y.astype(y_ref.dtype)
                xw = jnp.where(row_lo, w_t[l0:l0 + 1, :], w_t[l1:l1 + 1, :]) * x_t[ps, :]
                keep = jnp.where(row_lo, etot_t[l0:l0 + 1, :], etot_t[l1:l1 + 1, :])
                st_ref[d, pair] = hp * keep + _dot(xw.astype(BF16), bm)

    @pl.when(seq_end)
    def _():
        hff_ref[0] = st_ref[0]
        hfb_ref[0] = st_ref[1]


def _ssd(xbc, dt, par, dsk, h0, *, span, l_ctx, n_ctx_spans, n_ctx_seq):
    n_tok = xbc.shape[0]
    d_inner = N_GROUPS * HEADS_PER_GROUP * HEAD_DIM
    nc = span // CHUNK
    cps = l_ctx // CHUNK
    spc = span // l_ctx
    gps = N_GROUPS
    n_gsteps = N_GROUPS // gps
    gw = HEADS_PER_GROUP * HEAD_DIM * gps
    sw = D_STATE * gps
    b_cb = d_inner // sw
    c_cb = b_cb + n_gsteps
    st_block = (1, PAIRS_PER_GROUP * gps, 2 * HEAD_DIM, D_STATE)

    rf = lambda s, g, c: s * nc + c
    rb = lambda s, g, c: s * nc + (nc - 1 - c)

    def hfin_index(chunk_of):
        def index(s, g, c):
            is_ctx = s < n_ctx_spans
            s_e = jnp.minimum(s, n_ctx_spans - 1)
            g_e = jnp.where(is_ctx, g, n_gsteps - 1)
            c_e = jnp.where(is_ctx, c, nc - 1)
            return (s_e * spc + lax.div(chunk_of(c_e), cps), g_e, 0, 0)
        return index

    in_specs = []
    for r in (rf, rb):
        in_specs += [pl.BlockSpec((CHUNK, gw), lambda s, g, c, r=r: (r(s, g, c), g)),
                     pl.BlockSpec((CHUNK, sw), lambda s, g, c, r=r: (r(s, g, c), b_cb + g)),
                     pl.BlockSpec((CHUNK, sw), lambda s, g, c, r=r: (r(s, g, c), c_cb + g)),
                     pl.BlockSpec((CHUNK, LANES), lambda s, g, c, r=r: (r(s, g, c), 0))]
    in_specs += [pl.BlockSpec((8, LANES), lambda s, g, c: (0, 0)),
                 pl.BlockSpec((2, gw), lambda s, g, c: (0, g)),
                 pl.BlockSpec((1, 2) + st_block[1:], lambda s, g, c: (jnp.maximum(s - n_ctx_spans, 0), 0, g, 0, 0))]
    hf_shape = jax.ShapeDtypeStruct((n_ctx_seq, N_GROUPS * PAIRS_PER_GROUP, 2 * HEAD_DIM, D_STATE), F32)
    return pl.pallas_call(
        functools.partial(_ssd_kernel, n_ctx_spans=n_ctx_spans, chunks_per_ctx_seq=cps),
        out_shape=(jax.ShapeDtypeStruct((n_tok, d_inner), BF16), jax.ShapeDtypeStruct((n_tok, d_inner), BF16),
                   hf_shape, hf_shape),
        grid=(n_tok // span, n_gsteps, nc),
        in_specs=in_specs,
        out_specs=(pl.BlockSpec((CHUNK, gw), lambda s, g, c: (rf(s, g, c), g)),
                   pl.BlockSpec((CHUNK, gw), lambda s, g, c: (rb(s, g, c), g)),
                   pl.BlockSpec(st_block, hfin_index(lambda c: c)),
                   pl.BlockSpec(st_block, hfin_index(lambda c: nc - 1 - c))),
        scratch_shapes=[pltpu.VMEM((2,) + st_block[1:], F32)],
        compiler_params=_params("arbitrary", "arbitrary", "arbitrary"),
        name="ssd",
    )(xbc, xbc, xbc, dt, xbc, xbc, xbc, dt, par, dsk, h0)


def _merge_kernel(yf_ref, yb_ref, z_ref, g_ref, a_ref, x0_ref, mod_ref, nssd_ref, wso_ref, wo_ref,
                  npost_ref, nffn_ref, x1_ref, h2_ref):
    d = x0_ref.shape[1]
    z = z_ref[...].astype(F32)
    y = (yf_ref[...].astype(F32) + yb_ref[...].astype(F32)) * _silu(z)
    y = _rms(y, nssd_ref[...])
    b_out = _dot(y.astype(BF16), wso_ref[...])
    g = g_ref[...].astype(F32)
    merged = _sigmoid(g[:, :d]) * a_ref[...].astype(F32) + _sigmoid(g[:, d:]) * b_out
    mix = _dot(merged.astype(BF16), wo_ref[...])
    x1 = x0_ref[...] + mod_ref[0, 2:3, :] * _rms(mix, npost_ref[...])
    x1_ref[...] = x1
    h2 = _rms(x1, nffn_ref[...]) * (1.0 + mod_ref[0, 4:5, :]) + mod_ref[0, 3:4, :]
    h2_ref[...] = h2.astype(BF16)


def _merge(yf, yb, proj, a_out, x0, mods3, ssd_norm, wso, wo, npost, nffn, *, z_cb, g_cb, n_ctx, l_lat, n_lat):
    n_tok, d = x0.shape
    d_inner = yf.shape[1]
    tm = 256
    nct, tpl = n_ctx // tm, l_lat // tm
    row = lambda i: (i, 0)
    const = lambda i: (0, 0)
    return pl.pallas_call(
        _merge_kernel,
        out_shape=(jax.ShapeDtypeStruct((n_tok, d), F32), jax.ShapeDtypeStruct((n_tok, d), BF16)),
        grid=(n_tok // tm,),
        in_specs=[pl.BlockSpec((tm, d_inner), row),
                  pl.BlockSpec((tm, d_inner), row),
                  pl.BlockSpec((tm, d_inner), lambda i: (i, z_cb)),
                  pl.BlockSpec((tm, 2 * d), lambda i: (i, g_cb)),
                  pl.BlockSpec((tm, d), row),
                  pl.BlockSpec((tm, d), row),
                  pl.BlockSpec((1, 6, d), lambda i: (_mod_index(i, nct, tpl, n_lat), 0, 0)),
                  pl.BlockSpec((1, d_inner), const),
                  pl.BlockSpec((d_inner, d), const),
                  pl.BlockSpec((d, d), const),
                  pl.BlockSpec((1, d), const),
                  pl.BlockSpec((1, d), const)],
        out_specs=(pl.BlockSpec((tm, d), row), pl.BlockSpec((tm, d), row)),
        compiler_params=_params("arbitrary"),
        name="merge",
    )(yf, yb, proj, proj, a_out, x0, mods3, ssd_norm.reshape(1, d_inner), wso, wo,
      npost.reshape(1, d), nffn.reshape(1, d))


SUBLANES = 8
ROUTER_TOKEN_BLOCK = SUBLANES * LANES


def _sort_pairs(n):
    pairs = []
    p = 1
    while p < n:
        k = p
        while k >= 1:
            for j in range(k % p, n - k, 2 * k):
                for i in range(min(k, n - j - k)):
                    if (i + j) // (2 * p) == (i + j + k) // (2 * p):
                        pairs.append((i + j, i + j + k))
            k //= 2
        p *= 2
    return pairs


def _sort_desc(v):
    v = list(v)
    for a, b in _sort_pairs(len(v)):
        v[a], v[b] = jnp.maximum(v[a], v[b]), jnp.minimum(v[a], v[b])
    return v


def _merge_top(a, b):
    n = len(a)
    c = [jnp.maximum(a[i], b[n - 1 - i]) for i in range(n)]
    s = n // 2
    while s >= 1:
        for i in range(n):
            if i & s == 0:
                c[i], c[i + s] = jnp.maximum(c[i], c[i + s]), jnp.minimum(c[i], c[i + s])
        s //= 2
    return c


def _top_sorted(items, keep=TOPK):
    groups = [_sort_desc(items[g:g + keep]) for g in range(0, len(items), keep)]
    while len(groups) > 1:
        groups = [_merge_top(groups[i], groups[i + 1]) for i in range(0, len(groups), 2)]
    return groups[0]


_STAIRCASE = [(k, l) for k in range(TOPK) for l in range(TOPK) if (k + 1) * (l + 1) <= TOPK]
_STAIRCASE_PAD = 4 * TOPK - len(_STAIRCASE)


def _router_kernel(h2_ref, wq_ref, keys_ref, e1_ref, e2_ref, th_ref, s_ref, k8_ref):
    def to_row(v8):
        return jnp.concatenate([v8[r:r + 1, :] for r in range(SUBLANES)], axis=1)

    def head(h, carry):
        q = _dot(h2_ref[...], wq_ref[h])
        tops, maxes = [], []
        for half in range(2):
            s_t = _dot_nt(keys_ref[h, half], q[:, half * N_KEYS:(half + 1) * N_KEYS].astype(BF16))
            s_ref[half] = s_t
            for r in range(SUBLANES):
                k8_ref[half, pl.ds(r, N_KEYS, stride=SUBLANES), :] = s_t[:, r * LANES:(r + 1) * LANES]
            top = _top_sorted([k8_ref[half, k * SUBLANES:(k + 1) * SUBLANES, :] for k in range(N_KEYS)])
            maxes.append(top[0])
            tops.append([jnp.exp(t - top[0]) for t in top])
        a, b = tops
        cand = [a[k] * b[l] for k, l in _STAIRCASE]
        best = _top_sorted(cand + [jnp.full_like(cand[0], -1.0)] * _STAIRCASE_PAD)
        theta = best[-1]
        inv_z = 1.0 / functools.reduce(lambda u, v: u + v, best)
        a_n = [x * inv_z for x in a]
        th = None
        for (k, l), c in zip(_STAIRCASE, cand):
            v = jnp.where(c >= theta, b[l] * a_n[k], jnp.inf)
            th = v if th is None else jnp.minimum(th, v)
        e1_ref[h] = jnp.exp(s_ref[0] - to_row(maxes[0])) * to_row(inv_z)
        e2_ref[h] = jnp.exp(s_ref[1] - to_row(maxes[1]))
        th_ref[pl.ds(h, 1), :] = to_row(th)
        return carry

    lax.fori_loop(0, PEER_HEADS, head, 0)


def _router(h2, wq, keys):
    n_tok, d = h2.shape
    tb = ROUTER_TOKEN_BLOCK
    return pl.pallas_call(
        _router_kernel,
        out_shape=(jax.ShapeDtypeStruct((PEER_HEADS, N_KEYS, n_tok), F32),
                   jax.ShapeDtypeStruct((PEER_HEADS, N_KEYS, n_tok), F32),
                   jax.ShapeDtypeStruct((PEER_HEADS, n_tok), F32)),
        grid=(n_tok // tb,),
        in_specs=[pl.BlockSpec((tb, d), lambda i: (i, 0)),
                  pl.BlockSpec(wq.shape, lambda i: (0, 0, 0)),
                  pl.BlockSpec(keys.shape, lambda i: (0, 0, 0, 0))],
        out_specs=(pl.BlockSpec((PEER_HEADS, N_KEYS, tb), lambda i: (0, 0, i)),
                   pl.BlockSpec((PEER_HEADS, N_KEYS, tb), lambda i: (0, 0, i)),
                   pl.BlockSpec((PEER_HEADS, tb), lambda i: (0, i))),
        scratch_shapes=[pltpu.VMEM((2, N_KEYS, tb), F32), pltpu.VMEM((2, N_KEYS * SUBLANES, LANES), F32)],
        compiler_params=_params("arbitrary"),
        name="router",
    )(h2, wq, keys)


PEER_TOKEN_BLOCK = 512
PEER_EXPERT_CHUNK = 512
PEER_CHUNKS_PER_STEP = 4


GELU_C = math.sqrt(2.0 / math.pi)
GELU_K = 0.044715


def _gelu_tanh(x, c, k):
    return 0.5 * x * (1.0 + jnp.tanh(c * (x + k * (x * x * x))))


def _peer_weighted_act(e1_ref, i0, e2_ref, th_ref, gc_ref, at_ref, gw_ref):
    e_chunk, tb = at_ref.shape
    gc = gc_ref[0:1, :].astype(at_ref.dtype)
    gk = gc_ref[1:2, :].astype(at_ref.dtype)
    for ii in range(e_chunk // N_KEYS):
        rs = slice(ii * N_KEYS, (ii + 1) * N_KEYS)
        for lb in range(tb // LANES):
            ls = slice(lb * LANES, (lb + 1) * LANES)
            w = None
            for h in range(PEER_HEADS):
                p = e2_ref[h, :, ls] * e1_ref[h, i0 + ii:i0 + ii + 1, ls]
                sel = jnp.where(p >= th_ref[h:h + 1, ls], p, 0.0)
                w = sel if w is None else w + sel
            gw_ref[rs, ls] = w.astype(BF16) * _gelu_tanh(at_ref[rs, ls], gc, gk)


def _peer_kernel(h2_ref, u_ref, vt_ref, e1_ref, e2_ref, th_ref, gc_ref, x1_ref, mod_ref, npost_ref,
                 yc_ref, ys_ref, acc_ref, *bufs, n_it, n_ctx_blocks):
    it = pl.program_id(1)
    n_sub = PEER_CHUNKS_PER_STEP
    at_refs, gw_refs = bufs[:n_sub], bufs[n_sub:]
    ec = at_refs[0].shape[0]

    @pl.when(it == 0)
    def _():
        acc_ref[...] = jnp.zeros(acc_ref.shape, F32)

    h2 = h2_ref[...]

    def project(k):
        at_refs[k][...] = _dot_nt(u_ref[k * ec:(k + 1) * ec, :], h2).astype(BF16)

    def weigh(k):
        _peer_weighted_act(e1_ref, k * (ec // N_KEYS), e2_ref, th_ref, gc_ref, at_refs[k], gw_refs[k])

    def combine(k):
        acc_ref[...] += _dot(vt_ref[:, k * ec:(k + 1) * ec], gw_refs[k][...])

    project(0)
    for k in range(n_sub):
        if k + 1 < n_sub:
            project(k + 1)
        weigh(k)
        combine(k)

    last = it == n_it - 1
    is_ctx = pl.program_id(0) < n_ctx_blocks

    def result():
        o = acc_ref[...].T
        return x1_ref[...] + mod_ref[0, 5:6, :] * _rms(o, npost_ref[...])

    @pl.when(jnp.logical_and(last, is_ctx))
    def _():
        yc_ref[...] = result()

    @pl.when(jnp.logical_and(last, jnp.logical_not(is_ctx)))
    def _():
        ys_ref[...] = result()


def _peer(h2, u, vt, e1, e2, th, x1, mods3, npost, *, n_ctx, l_lat, n_lat):
    n_tok, d = h2.shape
    n_exp = u.shape[0]
    tb, ec, n_sub = PEER_TOKEN_BLOCK, PEER_EXPERT_CHUNK, PEER_CHUNKS_PER_STEP
    step_e = ec * n_sub
    n_it = n_exp // step_e
    rows_e1 = step_e // N_KEYS
    nct, tpl = n_ctx // tb, l_lat // tb
    gelu_consts = jnp.asarray(np.repeat(np.array([[GELU_C], [GELU_K]] + [[0.0]] * 6, np.float32), LANES, axis=1))
    return pl.pallas_call(
        functools.partial(_peer_kernel, n_it=n_it, n_ctx_blocks=nct),
        out_shape=(jax.ShapeDtypeStruct((n_ctx, d), F32), jax.ShapeDtypeStruct((n_tok - n_ctx, d), F32)),
        grid=(n_tok // tb, n_it),
        in_specs=[pl.BlockSpec((tb, d), lambda i, t: (i, 0)),
                  pl.BlockSpec((step_e, d), lambda i, t: (t, 0)),
                  pl.BlockSpec((d, step_e), lambda i, t: (0, t)),
                  pl.BlockSpec((PEER_HEADS, rows_e1, tb), lambda i, t: (0, t, i)),
                  pl.BlockSpec((PEER_HEADS, N_KEYS, tb), lambda i, t: (0, 0, i)),
                  pl.BlockSpec((PEER_HEADS, tb), lambda i, t: (0, i)),
                  pl.BlockSpec(gelu_consts.shape, lambda i, t: (0, 0)),
                  pl.BlockSpec((tb, d), lambda i, t: (i, 0)),
                  pl.BlockSpec((1, 6, d), lambda i, t: (_mod_index(i, nct, tpl, n_lat), 0, 0)),
                  pl.BlockSpec((1, d), lambda i, t: (0, 0))],
        out_specs=(pl.BlockSpec((tb, d), lambda i, t: (jnp.minimum(i, nct - 1), 0)),
                   pl.BlockSpec((tb, d), lambda i, t: (jnp.maximum(i - nct, 0), 0))),
        scratch_shapes=[pltpu.VMEM((d, tb), F32)] + [pltpu.VMEM((ec, tb), BF16) for _ in range(2 * n_sub)],
        compiler_params=_params("arbitrary", "arbitrary"),
        name="peer",
    )(h2, u, vt, e1, e2, th, gelu_consts, x1, mods3, npost.reshape(1, d))


def _grid_posemb(rows, d):
    r = np.repeat(np.arange(rows, dtype=np.float32), GRID_W)
    col = np.tile(np.arange(GRID_W, dtype=np.float32), rows)
    quarter = d // 4
    omega = (1.0 / (10000.0 ** (np.arange(quarter, dtype=np.float32) / np.float32(quarter)))).astype(np.float32)
    ar = r[:, None] * omega
    ac = col[:, None] * omega
    return np.concatenate([np.sin(ar), np.cos(ar), np.sin(ac), np.cos(ac)], axis=-1).astype(np.float32)


def kernel(x_prompt, x_sample, state_ssd, c, c_ctx, w_ada, b_ada, norm_mix_pre, norm_mix_post, norm_ffn_pre,
           norm_ffn_post, w_in, w_conv, b_conv, a_log, dt_bias, d_skip, ssd_norm, w_fourier, w_ssd_out, w_out,
           peer_query, peer_sub_keys, peer_u, peer_v):
    n_ctx_seq, l_ctx, d = x_prompt.shape
    n_lat, l_lat, _ = x_sample.shape
    depth = w_ada.shape[0]
    assert depth == 1
    n_ctx = n_ctx_seq * l_ctx
    span = l_lat
    assert span % l_ctx == 0 and n_ctx % span == 0 and n_ctx > 0 and n_lat > 0
    assert l_ctx & (l_ctx - 1) == 0 and span & (span - 1) == 0
    assert l_ctx % 256 == 0 and span % PEER_TOKEN_BLOCK == 0 and n_ctx % PEER_TOKEN_BLOCK == 0
    n_ctx_spans = n_ctx // span
    fd = FOURIER_GROUPS * FOURIER_GROUP_DIM
    d_inner = 2 * d
    n_heads = d_inner // HEAD_DIM
    assert n_heads == N_GROUPS * HEADS_PER_GROUP
    gn = N_GROUPS * D_STATE
    conv_dim = d_inner + 2 * gn
    col_z, col_xbc, col_dt = fd, fd + d_inner, fd + d_inner + conv_dim
    col_gate = col_dt + 2 * n_heads
    l = 0

    rows = n_lat + 1
    rows_pad = -(-rows // 8) * 8
    cond = jnp.concatenate([c, c_ctx[None, :], jnp.zeros((rows_pad - rows, d), F32)], axis=0)
    mods3 = _mods(cond, w_ada[l], b_ada[l]).reshape(rows_pad, 6, d)

    pos = jnp.asarray(_grid_posemb(l_lat // GRID_W, d))
    x0, hh, hl = _norm1(x_prompt.reshape(n_ctx, d), x_sample.reshape(n_lat * l_lat, d), pos, mods3,
                        norm_mix_pre[l], l_lat=l_lat, n_lat=n_lat)

    wi = w_in[l]
    w_main = jnp.concatenate([wi[:, col_z:col_xbc], wi[:, col_gate:], wi[:, :fd], wi[:, col_xbc:col_dt]],
                             axis=1).astype(BF16)
    proj = _mm(hh, w_main, BF16)
    z_cb, g_cb = 0, 1
    f_cb = (2 * d_inner) // fd
    xbc_col0 = 2 * d_inner + fd

    assert 2 * n_heads <= LANES
    wdt = jnp.pad(wi[:, col_dt:col_gate], ((0, 0), (0, LANES - 2 * n_heads)))
    wdt_h = wdt.astype(BF16)
    wdt_l = (wdt - wdt_h.astype(F32)).astype(BF16)
    dt_raw = _dt_proj(hh, hl, wdt_h, wdt_l)

    lanes = lambda p: jnp.pad(p.reshape(1, 2 * n_heads), ((0, 0), (0, LANES - 2 * n_heads)))
    par = jnp.concatenate([lanes(a_log[l]), lanes(dt_bias[l]), jnp.zeros((6, LANES), F32)], axis=0)
    dsk = jnp.repeat(d_skip[l], HEAD_DIM, axis=1)

    a_out = _fourier(proj, w_fourier[l].astype(BF16), span=span, l_ctx=l_ctx, n_ctx_spans=n_ctx_spans, col_block=f_cb)
    xbc = _conv(proj, w_conv[l], b_conv[l], span=span, l_ctx=l_ctx, n_ctx_spans=n_ctx_spans, col0=xbc_col0)
    h0 = state_ssd[:, l].reshape(n_lat, 2, N_GROUPS * PAIRS_PER_GROUP, 2 * HEAD_DIM, D_STATE)
    yf, yb, hff, hfb = _ssd(xbc, dt_raw, par, dsk, h0, span=span, l_ctx=l_ctx, n_ctx_spans=n_ctx_spans,
                            n_ctx_seq=n_ctx_seq)

    x1, h2 = _merge(yf, yb, proj, a_out, x0, mods3, ssd_norm[l], w_ssd_out[l].astype(BF16), w_out[l].astype(BF16),
                    norm_mix_post[l], norm_ffn_pre[l], z_cb=z_cb, g_cb=g_cb, n_ctx=n_ctx, l_lat=l_lat, n_lat=n_lat)

    wq = jnp.transpose(peer_query[l].reshape(d, PEER_HEADS, 2 * N_KEYS), (1, 0, 2)).astype(BF16)
    e1, e2, th = _router(h2, wq, peer_sub_keys[l].astype(BF16))
    yc, ys = _peer(h2, peer_u[l].astype(BF16), peer_v[l].T.astype(BF16), e1, e2, th, x1, mods3, norm_ffn_post[l],
                   n_ctx=n_ctx, l_lat=l_lat, n_lat=n_lat)

    hfin = jnp.stack([hff, hfb], axis=1)
    new_state = hfin.reshape(n_ctx_seq, 1, 2, n_heads, HEAD_DIM, D_STATE).astype(x_prompt.dtype)
    return (yc.reshape(n_ctx_seq, l_ctx, d), ys.reshape(n_lat, l_lat, d), new_state)
```

```python
import functools
import math

import numpy as np
import jax
import jax.numpy as jnp
from jax import lax
from jax.experimental import pallas as pl
from jax.experimental.pallas import tpu as pltpu

F32 = jnp.float32
BF16 = jnp.bfloat16

EPS = 1e-6
GRID_W = 64
FOURIER_GROUPS = 8
FOURIER_GROUP_DIM = 128
HEAD_DIM = 64
N_GROUPS = 4
HEADS_PER_GROUP = 8
D_STATE = 128
CONV_W = 5
CHUNK = 128
PEER_HEADS = 8
N_KEYS = 128
TOPK = 16

V7X_VMEM_BYTES = 64 * 1024 * 1024
VMEM_LIMIT_BYTES = V7X_VMEM_BYTES - 8 * 1024 * 1024
LANES = 128

_NT = (((1,), (1,)), ((), ()))


def _params(*sem, flags=None):
    return pltpu.CompilerParams(dimension_semantics=sem, vmem_limit_bytes=VMEM_LIMIT_BYTES, flags=flags)


def _pick(n, candidates):
    for c in candidates:
        if n % c == 0:
            return c
    raise ValueError(f"no tile in {candidates} divides {n}")


def _dot(a, b):
    return jnp.dot(a, b, preferred_element_type=F32)


def _dot_nt(a, b):
    return lax.dot_general(a, b, _NT, preferred_element_type=F32)


def _split3(x):
    x1 = x.astype(BF16)
    r1 = x - x1.astype(F32)
    x2 = r1.astype(BF16)
    x3 = (r1 - x2.astype(F32)).astype(BF16)
    return x1, x2, x3


def _sigmoid(x):
    return 1.0 / (1.0 + jnp.exp(-x))


def _silu(x):
    return x * _sigmoid(x)


def _rms(x, w):
    return x * lax.rsqrt(jnp.mean(x * x, axis=-1, keepdims=True) + EPS) * w


def _mods_kernel(c_ref, w_ref, b_ref, o_ref):
    s = _silu(c_ref[...])
    s1, s2, s3 = _split3(s)
    w1, w2, w3 = _split3(w_ref[...])
    acc = _dot(s1, w1) + (_dot(s1, w2) + _dot(s2, w1)) + (_dot(s1, w3) + _dot(s2, w2) + _dot(s3, w1))
    o_ref[...] = acc + b_ref[...]


def _mods(cond, w_ada, b_ada):
    rows, d = cond.shape
    n = w_ada.shape[1]
    tn = _pick(n, (1024, 512, 256, 128))
    return pl.pallas_call(
        _mods_kernel,
        out_shape=jax.ShapeDtypeStruct((rows, n), F32),
        grid=(n // tn,),
        in_specs=[pl.BlockSpec((rows, d), lambda j: (0, 0)),
                  pl.BlockSpec((d, tn), lambda j: (0, j)),
                  pl.BlockSpec((1, tn), lambda j: (0, j))],
        out_specs=pl.BlockSpec((rows, tn), lambda j: (0, j)),
        compiler_params=_params("arbitrary"),
        name="mods",
    )(cond, w_ada, b_ada.reshape(1, n))


def _mod_index(tile, n_ctx_tiles, tiles_per_lat, n_lat):
    t = jnp.maximum(tile - n_ctx_tiles, 0)
    return jnp.where(tile < n_ctx_tiles, n_lat, lax.div(t, tiles_per_lat))


def _norm1_kernel(xc_ref, xs_ref, pos_ref, mod_ref, w_ref, x0_ref, hh_ref, hl_ref, *, n_ctx_tiles):
    i = pl.program_id(0)
    x = jnp.where(i < n_ctx_tiles, xc_ref[...], xs_ref[...] + pos_ref[...])
    x0_ref[...] = x
    h = _rms(x, w_ref[...]) * (1.0 + mod_ref[0, 1:2, :]) + mod_ref[0, 0:1, :]
    hh = h.astype(BF16)
    hh_ref[...] = hh
    hl_ref[...] = (h - hh.astype(F32)).astype(BF16)


def _norm1(xc, xs, pos, mods3, w, *, l_lat, n_lat):
    n_ctx, d = xc.shape
    n_tok = n_ctx + xs.shape[0]
    tm = 256
    nct = n_ctx // tm
    tpl = l_lat // tm
    return pl.pallas_call(
        functools.partial(_norm1_kernel, n_ctx_tiles=nct),
        out_shape=(jax.ShapeDtypeStruct((n_tok, d), F32),
                   jax.ShapeDtypeStruct((n_tok, d), BF16),
                   jax.ShapeDtypeStruct((n_tok, d), BF16)),
        grid=(n_tok // tm,),
        in_specs=[pl.BlockSpec((tm, d), lambda i: (jnp.minimum(i, nct - 1), 0)),
                  pl.BlockSpec((tm, d), lambda i: (jnp.maximum(i - nct, 0), 0)),
                  pl.BlockSpec((tm, d), lambda i: (lax.rem(jnp.maximum(i - nct, 0), tpl), 0)),
                  pl.BlockSpec((1, 6, d), lambda i: (_mod_index(i, nct, tpl, n_lat), 0, 0)),
                  pl.BlockSpec((1, d), lambda i: (0, 0))],
        out_specs=(pl.BlockSpec((tm, d), lambda i: (i, 0)),
                   pl.BlockSpec((tm, d), lambda i: (i, 0)),
                   pl.BlockSpec((tm, d), lambda i: (i, 0))),
        compiler_params=_params("arbitrary"),
        name="norm1",
    )(xc, xs, pos, mods3, w.reshape(1, d))


def _mm_kernel(a_ref, b_ref, o_ref):
    o_ref[...] = _dot(a_ref[...], b_ref[...]).astype(o_ref.dtype)


def _mm(a, b, out_dtype):
    m, k = a.shape
    n = b.shape[1]
    tm = _pick(m, (1024, 512, 256))
    tn = _pick(n, (1024, 512, 256, 128))
    return pl.pallas_call(
        _mm_kernel,
        out_shape=jax.ShapeDtypeStruct((m, n), out_dtype),
        grid=(m // tm, n // tn),
        in_specs=[pl.BlockSpec((tm, k), lambda i, j: (i, 0)),
                  pl.BlockSpec((k, tn), lambda i, j: (0, j))],
        out_specs=pl.BlockSpec((tm, tn), lambda i, j: (i, j)),
        compiler_params=_params("arbitrary", "arbitrary"),
        name="mm",
    )(a, b)


def _dt_kernel(hh_ref, hl_ref, wh_ref, wl_ref, o_ref):
    hh = hh_ref[...]
    wh = wh_ref[...]
    o_ref[...] = _dot(hh, wh) + (_dot(hh, wl_ref[...]) + _dot(hl_ref[...], wh))


def _dt_proj(hh, hl, wh, wl):
    m, k = hh.shape
    n = wh.shape[1]
    tm = _pick(m, (1024, 512, 256))
    return pl.pallas_call(
        _dt_kernel,
        out_shape=jax.ShapeDtypeStruct((m, n), F32),
        grid=(m // tm,),
        in_specs=[pl.BlockSpec((tm, k), lambda i: (i, 0)),
                  pl.BlockSpec((tm, k), lambda i: (i, 0)),
                  pl.BlockSpec((k, n), lambda i: (0, 0)),
                  pl.BlockSpec((k, n), lambda i: (0, 0))],
        out_specs=pl.BlockSpec((tm, n), lambda i: (i, 0)),
        compiler_params=_params("arbitrary"),
        name="dt_proj",
    )(hh, hl, wh, wl)


def _dft_mats(n):
    idx = np.arange(n, dtype=np.int64)
    ang = 2.0 * np.pi * ((idx[:, None] * idx[None, :]) % n).astype(np.float64) / n
    scale = 1.0 / math.sqrt(n)
    return np.cos(ang) * scale, np.sin(ang) * scale


def _span_dft(span, seq_len):
    c, s = _dft_mats(seq_len)
    eye = np.eye(span // seq_len)
    return np.kron(eye, c), np.kron(eye, s)


def _fourier_kernel(u_ref, cs_ref, cl_ref, sl_ref, wf_ref, o_ref, csb_ref, clb_ref, slb_ref, *, n_ctx_spans):
    s = pl.program_id(0)

    @pl.when(jnp.logical_or(s == 0, s == n_ctx_spans))
    def _():
        csb_ref[...] = cs_ref[...].astype(BF16)
        clb_ref[...] = cl_ref[0].astype(BF16)
        slb_ref[...] = sl_ref[0].astype(BF16)

    u = u_ref[...]
    pc, ps = [], []
    for g in range(FOURIER_GROUPS):
        p = _dot(u[:, g * FOURIER_GROUP_DIM:(g + 1) * FOURIER_GROUP_DIM], csb_ref[...])
        pc.append(p[:, :FOURIER_GROUP_DIM])
        ps.append(p[:, FOURIER_GROUP_DIM:])
    pc = jnp.concatenate(pc, axis=1).astype(BF16)
    ps = jnp.concatenate(ps, axis=1).astype(BF16)
    f = _dot(clb_ref[...], pc) - _dot(slb_ref[...], ps)
    o_ref[...] = _dot(f.astype(BF16), wf_ref[...]).astype(o_ref.dtype)


def _fourier(proj, w_fourier, *, span, l_ctx, n_ctx_spans, col_block):
    n_tok = proj.shape[0]
    fd = FOURIER_GROUPS * FOURIER_GROUP_DIM
    d = w_fourier.shape[1]
    cc, sc = _dft_mats(FOURIER_GROUP_DIM)
    cs = np.concatenate([cc, sc], axis=1)
    c_ctx, s_ctx = _span_dft(span, l_ctx)
    c_lat, s_lat = _dft_mats(span)
    cl = jnp.asarray(np.stack([c_ctx, c_lat]), F32)
    sl = jnp.asarray(np.stack([s_ctx, s_lat]), F32)
    kind = lambda s: (jnp.where(s < n_ctx_spans, 0, 1), 0, 0)
    return pl.pallas_call(
        functools.partial(_fourier_kernel, n_ctx_spans=n_ctx_spans),
        out_shape=jax.ShapeDtypeStruct((n_tok, d), BF16),
        grid=(n_tok // span,),
        in_specs=[pl.BlockSpec((span, fd), lambda s: (s, col_block)),
                  pl.BlockSpec(cs.shape, lambda s: (0, 0)),
                  pl.BlockSpec((1, span, span), kind),
                  pl.BlockSpec((1, span, span), kind),
                  pl.BlockSpec((fd, d), lambda s: (0, 0))],
        out_specs=pl.BlockSpec((span, d), lambda s: (s, 0)),
        scratch_shapes=[pltpu.VMEM(cs.shape, BF16), pltpu.VMEM((span, span), BF16), pltpu.VMEM((span, span), BF16)],
        compiler_params=_params("arbitrary"),
        name="fourier",
    )(proj, jnp.asarray(cs, F32), cl, sl, w_fourier)


def _conv_kernel(u_ref, w_ref, b_ref, o_ref, *, n_ctx_spans, l_ctx):
    n = u_ref.shape[0]
    half = CONV_W // 2
    seq_len = jnp.where(pl.program_id(0) < n_ctx_spans, l_ctx, n)

    def taps(u, pos=None):
        rows = u.shape[0]
        acc = b_ref[...] + u * w_ref[half:half + 1, :]
        for k in range(CONV_W):
            off = k - half
            if off == 0:
                continue
            shifted = pltpu.roll(u, (-off) % rows, axis=0)
            if pos is not None:
                shifted = jnp.where(jnp.logical_and(pos + off >= 0, pos + off < seq_len), shifted, 0.0)
            acc = acc + shifted * w_ref[k:k + 1, :]
        return _silu(acc).astype(o_ref.dtype)

    o_ref[...] = taps(u_ref[...].astype(F32))
    slab, fix = 4 * SUBLANES, 2 * SUBLANES
    for b in range(0, n + 1, l_ctx):
        lo = min(max(b - slab // 2, 0), n - slab)
        w0 = max(b - fix // 2, 0)
        w1 = min(b + fix // 2, n)
        pos = jnp.bitwise_and(lo + lax.broadcasted_iota(jnp.int32, (slab, u_ref.shape[1]), 0), seq_len - 1)
        redo = taps(u_ref[lo:lo + slab, :].astype(F32), pos)
        o_ref[w0:w1, :] = redo[w0 - lo:w1 - lo, :]


def _conv(proj, w_conv, b_conv, *, span, l_ctx, n_ctx_spans, col0):
    n_tok = proj.shape[0]
    cdim = w_conv.shape[1]
    tc = 512
    cb0 = col0 // tc
    return pl.pallas_call(
        functools.partial(_conv_kernel, n_ctx_spans=n_ctx_spans, l_ctx=l_ctx),
        out_shape=jax.ShapeDtypeStruct((n_tok, cdim), BF16),
        grid=(n_tok // span, cdim // tc),
        in_specs=[pl.BlockSpec((span, tc), lambda s, j: (s, cb0 + j)),
                  pl.BlockSpec((CONV_W, tc), lambda s, j: (0, j)),
                  pl.BlockSpec((1, tc), lambda s, j: (0, j))],
        out_specs=pl.BlockSpec((span, tc), lambda s, j: (s, j)),
        compiler_params=_params("arbitrary", "arbitrary"),
        name="conv",
    )(proj, w_conv, b_conv.reshape(1, cdim))


PAIRS_PER_GROUP = HEADS_PER_GROUP // 2


def _ssd_kernel(xf_ref, bf_ref, cf_ref, dtf_ref, xb_ref, bb_ref, cb_ref, dtb_ref, par_ref, dsk_ref, h0_ref,
                yf_ref, yb_ref, hf_ref, st_ref, *, n_ctx_spans, chunks_per_ctx_seq, n_chunks):
    s = pl.program_id(0)
    c = pl.program_id(2)
    is_ctx = s < n_ctx_spans
    cps = chunks_per_ctx_seq
    seq_start = jnp.where(is_ctx, lax.rem(c, cps) == 0, c == 0)
    seq_end = jnp.logical_and(is_ctx, lax.rem(c, cps) == cps - 1)

    @pl.when(jnp.logical_and(seq_start, is_ctx))
    def _():
        st_ref[...] = jnp.zeros(st_ref.shape, F32)

    @pl.when(jnp.logical_and(seq_start, jnp.logical_not(is_ctx)))
    def _():
        st_ref[...] = h0_ref[0]

    li = lax.broadcasted_iota(jnp.int32, (CHUNK, CHUNK), 0)
    si = lax.broadcasted_iota(jnp.int32, (CHUNK, CHUNK), 1)
    lane_lo = si < HEAD_DIM
    row_lo = li < HEAD_DIM
    gw = HEADS_PER_GROUP * HEAD_DIM
    n_groups = N_GROUPS

    for d, (x_ref, b_ref, c_ref, dt_ref, y_ref) in enumerate(((xf_ref, bf_ref, cf_ref, dtf_ref, yf_ref),
                                                                 (xb_ref, bb_ref, cb_ref, dtb_ref, yb_ref))):
        mask = (si <= li) if d == 0 else (si >= li)
        mb = mask.astype(BF16)
        raw = dt_ref[...] + par_ref[1:2, :]
        dt = jnp.maximum(raw, 0.0) + jnp.log1p(jnp.exp(-jnp.abs(raw)))
        da = dt * (-jnp.exp(par_ref[0:1, :]))
        d1, d2, d3 = _split3(da)
        cs = _dot(mb, d1) + _dot(mb, d2) + _dot(mb, d3)
        tot = cs[CHUNK - 1:CHUNK, :] if d == 0 else cs[0:1, :]
        w_t = (dt * jnp.exp(tot - cs)).T
        cs_t = cs.T
        dt_t = dt.T
        ecs = jnp.exp(cs)
        etot_t = jnp.exp(cs_t[:, CHUNK - 1:CHUNK] if d == 0 else cs_t[:, 0:1])
        for gi in range(n_groups):
            gl = slice(gi * LANES, (gi + 1) * LANES)
            bm = b_ref[:, gl]
            cm = c_ref[:, gl]
            cb = _dot_nt(cm, bm)
            x = x_ref[:, gi * gw:(gi + 1) * gw]
            x_t = x.astype(F32).T
            for kp in range(PAIRS_PER_GROUP):
                l0 = (d * n_groups + gi) * HEADS_PER_GROUP + 2 * kp
                l1 = l0 + 1
                ps = slice(kp * LANES, (kp + 1) * LANES)
                ys = slice(gi * gw + kp * LANES, gi * gw + (kp + 1) * LANES)
                pair = gi * PAIRS_PER_GROUP + kp
                xp = x[:, ps]
                lms = []
                for ln in (l0, l1):
                    seg = cs[:, ln:ln + 1] - cs_t[ln:ln + 1, :]
                    dec = jnp.exp(jnp.where(mask, seg, -jnp.inf))
                    lms.append((cb * dec * dt_t[ln:ln + 1, :]).astype(BF16))
                hp = st_ref[d, pair]
                y = jnp.where(lane_lo, _dot(lms[0], xp), _dot(lms[1], xp))
                y = y + _dot_nt(cm, hp.astype(BF16)) * jnp.where(lane_lo, ecs[:, l0:l0 + 1], ecs[:, l1:l1 + 1])
                y = y + dsk_ref[d:d + 1, ys] * xp.astype(F32)
                y_ref[:, ys] = y.astype(y_ref.dtype)
                xw = jnp.where(row_lo, w_t[l0:l0 + 1, :], w_t[l1:l1 + 1, :]) * x_t[ps, :]
                keep = jnp.where(row_lo, etot_t[l0:l0 + 1, :], etot_t[l1:l1 + 1, :])
                st_ref[d, pair] = hp * keep + _dot(xw.astype(BF16), bm)

    @pl.when(seq_end)
    def _():
        hf_ref[lax.div(c, cps), 0] = st_ref[0]
        hf_ref[lax.div(n_chunks - 1 - c, cps), 1] = st_ref[1]


def _ssd(xbc, dt, par, dsk, h0, *, span, l_ctx, n_ctx_spans, n_ctx_seq):
    n_tok = xbc.shape[0]
    d_inner = N_GROUPS * HEADS_PER_GROUP * HEAD_DIM
    nc = span // CHUNK
    cps = l_ctx // CHUNK
    spc = span // l_ctx
    gps = N_GROUPS
    n_gsteps = N_GROUPS // gps
    gw = HEADS_PER_GROUP * HEAD_DIM * gps
    sw = D_STATE * gps
    b_cb = d_inner // sw
    c_cb = b_cb + n_gsteps
    st_block = (1, PAIRS_PER_GROUP * gps, 2 * HEAD_DIM, D_STATE)

    rf = lambda s, g, c: s * nc + c
    rb = lambda s, g, c: s * nc + (nc - 1 - c)

    hf_block = (spc, 2) + st_block[1:]
    hf_index = lambda s, g, c: (jnp.minimum(s, n_ctx_spans - 1), 0, 0, 0, 0)

    in_specs = []
    for r in (rf, rb):
        in_specs += [pl.BlockSpec((CHUNK, gw), lambda s, g, c, r=r: (r(s, g, c), g)),
                     pl.BlockSpec((CHUNK, sw), lambda s, g, c, r=r: (r(s, g, c), b_cb + g)),
                     pl.BlockSpec((CHUNK, sw), lambda s, g, c, r=r: (r(s, g, c), c_cb + g)),
                     pl.BlockSpec((CHUNK, LANES), lambda s, g, c, r=r: (r(s, g, c), 0))]
    in_specs += [pl.BlockSpec((8, LANES), lambda s, g, c: (0, 0)),
                 pl.BlockSpec((2, gw), lambda s, g, c: (0, g)),
                 pl.BlockSpec((1, 2) + st_block[1:], lambda s, g, c: (jnp.maximum(s - n_ctx_spans, 0), 0, g, 0, 0))]
    hf_shape = jax.ShapeDtypeStruct((n_ctx_seq, 2, N_GROUPS * PAIRS_PER_GROUP, 2 * HEAD_DIM, D_STATE), F32)
    return pl.pallas_call(
        functools.partial(_ssd_kernel, n_ctx_spans=n_ctx_spans, chunks_per_ctx_seq=cps, n_chunks=nc),
        out_shape=(jax.ShapeDtypeStruct((n_tok, d_inner), BF16), jax.ShapeDtypeStruct((n_tok, d_inner), BF16),
                   hf_shape),
        grid=(n_tok // span, n_gsteps, nc),
        in_specs=in_specs,
        out_specs=(pl.BlockSpec((CHUNK, gw), lambda s, g, c: (rf(s, g, c), g)),
                   pl.BlockSpec((CHUNK, gw), lambda s, g, c: (rb(s, g, c), g)),
                   pl.BlockSpec(hf_block, hf_index)),
        scratch_shapes=[pltpu.VMEM((2,) + st_block[1:], F32)],
        compiler_params=_params("arbitrary", "arbitrary", "arbitrary"),
        name="ssd",
    )(xbc, xbc, xbc, dt, xbc, xbc, xbc, dt, par, dsk, h0)


def _merge_kernel(yf_ref, yb_ref, z_ref, g_ref, a_ref, x0_ref, mod_ref, nssd_ref, wso_ref, wo_ref,
                  npost_ref, nffn_ref, x1_ref, h2_ref):
    d = x0_ref.shape[1]
    z = z_ref[...].astype(F32)
    y = (yf_ref[...].astype(F32) + yb_ref[...].astype(F32)) * _silu(z)
    y = _rms(y, nssd_ref[...])
    b_out = _dot(y.astype(BF16), wso_ref[...])
    g = g_ref[...].astype(F32)
    merged = _sigmoid(g[:, :d]) * a_ref[...].astype(F32) + _sigmoid(g[:, d:]) * b_out
    mix = _dot(merged.astype(BF16), wo_ref[...])
    x1 = x0_ref[...] + mod_ref[0, 2:3, :] * _rms(mix, npost_ref[...])
    x1_ref[...] = x1
    h2 = _rms(x1, nffn_ref[...]) * (1.0 + mod_ref[0, 4:5, :]) + mod_ref[0, 3:4, :]
    h2_ref[...] = h2.astype(BF16)


def _merge(yf, yb, proj, a_out, x0, mods3, ssd_norm, wso, wo, npost, nffn, *, z_cb, g_cb, n_ctx, l_lat, n_lat):
    n_tok, d = x0.shape
    d_inner = yf.shape[1]
    tm = 256
    nct, tpl = n_ctx // tm, l_lat // tm
    row = lambda i: (i, 0)
    const = lambda i: (0, 0)
    return pl.pallas_call(
        _merge_kernel,
        out_shape=(jax.ShapeDtypeStruct((n_tok, d), F32), jax.ShapeDtypeStruct((n_tok, d), BF16)),
        grid=(n_tok // tm,),
        in_specs=[pl.BlockSpec((tm, d_inner), row),
                  pl.BlockSpec((tm, d_inner), row),
                  pl.BlockSpec((tm, d_inner), lambda i: (i, z_cb)),
                  pl.BlockSpec((tm, 2 * d), lambda i: (i, g_cb)),
                  pl.BlockSpec((tm, d), row),
                  pl.BlockSpec((tm, d), row),
                  pl.BlockSpec((1, 6, d), lambda i: (_mod_index(i, nct, tpl, n_lat), 0, 0)),
                  pl.BlockSpec((1, d_inner), const),
                  pl.BlockSpec((d_inner, d), const),
                  pl.BlockSpec((d, d), const),
                  pl.BlockSpec((1, d), const),
                  pl.BlockSpec((1, d), const)],
        out_specs=(pl.BlockSpec((tm, d), row), pl.BlockSpec((tm, d), row)),
        compiler_params=_params("arbitrary"),
        name="merge",
    )(yf, yb, proj, proj, a_out, x0, mods3, ssd_norm.reshape(1, d_inner), wso, wo,
      npost.reshape(1, d), nffn.reshape(1, d))


SUBLANES = 8
ROUTER_TOKEN_BLOCK = SUBLANES * LANES


def _sort_pairs(n):
    pairs = []
    p = 1
    while p < n:
        k = p
        while k >= 1:
            for j in range(k % p, n - k, 2 * k):
                for i in range(min(k, n - j - k)):
                    if (i + j) // (2 * p) == (i + j + k) // (2 * p):
                        pairs.append((i + j, i + j + k))
            k //= 2
        p *= 2
    return pairs


def _sort_desc(v):
    v = list(v)
    for a, b in _sort_pairs(len(v)):
        v[a], v[b] = jnp.maximum(v[a], v[b]), jnp.minimum(v[a], v[b])
    return v


def _merge_top(a, b):
    n = len(a)
    c = [jnp.maximum(a[i], b[n - 1 - i]) for i in range(n)]
    s = n // 2
    while s >= 1:
        for i in range(n):
            if i & s == 0:
                c[i], c[i + s] = jnp.maximum(c[i], c[i + s]), jnp.minimum(c[i], c[i + s])
        s //= 2
    return c


def _top_sorted(items, keep=TOPK):
    groups = [_sort_desc(items[g:g + keep]) for g in range(0, len(items), keep)]
    while len(groups) > 1:
        groups = [_merge_top(groups[i], groups[i + 1]) for i in range(0, len(groups), 2)]
    return groups[0]


_STAIRCASE = [(k, l) for k in range(TOPK) for l in range(TOPK) if (k + 1) * (l + 1) <= TOPK]
_STAIRCASE_PAD = 4 * TOPK - len(_STAIRCASE)


def _router_kernel(h2_ref, wq_ref, keys_ref, e1_ref, e2_ref, th_ref, s_ref, k8_ref):
    def to_row(v8):
        return jnp.concatenate([v8[r:r + 1, :] for r in range(SUBLANES)], axis=1)

    def head(h, carry):
        q = _dot(h2_ref[...], wq_ref[h])
        tops, maxes = [], []
        for half in range(2):
            s_t = _dot_nt(keys_ref[h, half], q[:, half * N_KEYS:(half + 1) * N_KEYS].astype(BF16))
            s_ref[half] = s_t
            for r in range(SUBLANES):
                k8_ref[half, pl.ds(r, N_KEYS, stride=SUBLANES), :] = s_t[:, r * LANES:(r + 1) * LANES]
            top = _top_sorted([k8_ref[half, k * SUBLANES:(k + 1) * SUBLANES, :] for k in range(N_KEYS)])
            maxes.append(top[0])
            tops.append([jnp.exp(t - top[0]) for t in top])
        a, b = tops
        cand = [a[k] * b[l] for k, l in _STAIRCASE]
        best = _top_sorted(cand + [jnp.full_like(cand[0], -1.0)] * _STAIRCASE_PAD)
        theta = best[-1]
        inv_z = 1.0 / functools.reduce(lambda u, v: u + v, best)
        a_n = [x * inv_z for x in a]
        th = None
        for (k, l), c in zip(_STAIRCASE, cand):
            v = jnp.where(c >= theta, b[l] * a_n[k], jnp.inf)
            th = v if th is None else jnp.minimum(th, v)
        e1_ref[h] = jnp.exp(s_ref[0] - to_row(maxes[0])) * to_row(inv_z)
        e2_ref[h] = jnp.exp(s_ref[1] - to_row(maxes[1]))
        th_ref[pl.ds(h, 1), :] = to_row(th)
        return carry

    lax.fori_loop(0, PEER_HEADS, head, 0)


def _router(h2, wq, keys):
    n_tok, d = h2.shape
    tb = ROUTER_TOKEN_BLOCK
    return pl.pallas_call(
        _router_kernel,
        out_shape=(jax.ShapeDtypeStruct((PEER_HEADS, N_KEYS, n_tok), F32),
                   jax.ShapeDtypeStruct((PEER_HEADS, N_KEYS, n_tok), F32),
                   jax.ShapeDtypeStruct((PEER_HEADS, n_tok), F32)),
        grid=(n_tok // tb,),
        in_specs=[pl.BlockSpec((tb, d), lambda i: (i, 0)),
                  pl.BlockSpec(wq.shape, lambda i: (0, 0, 0)),
                  pl.BlockSpec(keys.shape, lambda i: (0, 0, 0, 0))],
        out_specs=(pl.BlockSpec((PEER_HEADS, N_KEYS, tb), lambda i: (0, 0, i)),
                   pl.BlockSpec((PEER_HEADS, N_KEYS, tb), lambda i: (0, 0, i)),
                   pl.BlockSpec((PEER_HEADS, tb), lambda i: (0, i))),
        scratch_shapes=[pltpu.VMEM((2, N_KEYS, tb), F32), pltpu.VMEM((2, N_KEYS * SUBLANES, LANES), F32)],
        compiler_params=_params("arbitrary"),
        name="router",
    )(h2, wq, keys)


PEER_TOKEN_BLOCK = 512
PEER_EXPERT_CHUNK = 512
PEER_CHUNKS_PER_STEP = 4


GELU_C = math.sqrt(2.0 / math.pi)
GELU_K = 0.044715


def _gelu_tanh(x, c, k):
    return 0.5 * x * (1.0 + jnp.tanh(c * (x + k * (x * x * x))))


def _peer_weighted_act(e1_ref, i0, e2_ref, th_ref, gc_ref, at_ref, gw_ref):
    e_chunk, tb = at_ref.shape
    gc = gc_ref[0:1, :].astype(at_ref.dtype)
    gk = gc_ref[1:2, :].astype(at_ref.dtype)
    for ii in range(e_chunk // N_KEYS):
        rs = slice(ii * N_KEYS, (ii + 1) * N_KEYS)
        for lb in range(tb // LANES):
            ls = slice(lb * LANES, (lb + 1) * LANES)
            w = None
            for h in range(PEER_HEADS):
                p = e2_ref[h, :, ls] * e1_ref[h, i0 + ii:i0 + ii + 1, ls]
                sel = jnp.where(p >= th_ref[h:h + 1, ls], p, 0.0)
                w = sel if w is None else w + sel
            gw_ref[rs, ls] = w.astype(BF16) * _gelu_tanh(at_ref[rs, ls], gc, gk)


def _peer_kernel(h2_ref, u_ref, vt_ref, e1_ref, e2_ref, th_ref, gc_ref, x1_ref, mod_ref, npost_ref,
                 yc_ref, ys_ref, acc_ref, *bufs, n_it, n_ctx_blocks):
    it = pl.program_id(1)
    n_sub = PEER_CHUNKS_PER_STEP
    at_refs, gw_refs = bufs[:n_sub], bufs[n_sub:]
    ec = at_refs[0].shape[0]

    @pl.when(it == 0)
    def _():
        acc_ref[...] = jnp.zeros(acc_ref.shape, F32)

    h2 = h2_ref[...]

    def project(k):
        at_refs[k][...] = _dot_nt(u_ref[k * ec:(k + 1) * ec, :], h2).astype(BF16)

    def weigh(k):
        _peer_weighted_act(e1_ref, k * (ec // N_KEYS), e2_ref, th_ref, gc_ref, at_refs[k], gw_refs[k])

    def combine(k):
        acc_ref[...] += _dot(vt_ref[:, k * ec:(k + 1) * ec], gw_refs[k][...])

    project(0)
    for k in range(n_sub):
        if k + 1 < n_sub:
            project(k + 1)
        weigh(k)
        combine(k)

    last = it == n_it - 1
    is_ctx = pl.program_id(0) < n_ctx_blocks

    def result():
        o = acc_ref[...].T
        return x1_ref[...] + mod_ref[0, 5:6, :] * _rms(o, npost_ref[...])

    @pl.when(jnp.logical_and(last, is_ctx))
    def _():
        yc_ref[...] = result()

    @pl.when(jnp.logical_and(last, jnp.logical_not(is_ctx)))
    def _():
        ys_ref[...] = result()


def _peer(h2, u, vt, e1, e2, th, x1, mods3, npost, *, n_ctx, l_lat, n_lat):
    n_tok, d = h2.shape
    n_exp = u.shape[0]
    tb, ec, n_sub = PEER_TOKEN_BLOCK, PEER_EXPERT_CHUNK, PEER_CHUNKS_PER_STEP
    step_e = ec * n_sub
    n_it = n_exp // step_e
    rows_e1 = step_e // N_KEYS
    nct, tpl = n_ctx // tb, l_lat // tb
    gelu_consts = jnp.asarray(np.repeat(np.array([[GELU_C], [GELU_K]] + [[0.0]] * 6, np.float32), LANES, axis=1))
    return pl.pallas_call(
        functools.partial(_peer_kernel, n_it=n_it, n_ctx_blocks=nct),
        out_shape=(jax.ShapeDtypeStruct((n_ctx, d), F32), jax.ShapeDtypeStruct((n_tok - n_ctx, d), F32)),
        grid=(n_tok // tb, n_it),
        in_specs=[pl.BlockSpec((tb, d), lambda i, t: (i, 0)),
                  pl.BlockSpec((step_e, d), lambda i, t: (t, 0)),
                  pl.BlockSpec((d, step_e), lambda i, t: (0, t)),
                  pl.BlockSpec((PEER_HEADS, rows_e1, tb), lambda i, t: (0, t, i)),
                  pl.BlockSpec((PEER_HEADS, N_KEYS, tb), lambda i, t: (0, 0, i)),
                  pl.BlockSpec((PEER_HEADS, tb), lambda i, t: (0, i)),
                  pl.BlockSpec(gelu_consts.shape, lambda i, t: (0, 0)),
                  pl.BlockSpec((tb, d), lambda i, t: (i, 0)),
                  pl.BlockSpec((1, 6, d), lambda i, t: (_mod_index(i, nct, tpl, n_lat), 0, 0)),
                  pl.BlockSpec((1, d), lambda i, t: (0, 0))],
        out_specs=(pl.BlockSpec((tb, d), lambda i, t: (jnp.minimum(i, nct - 1), 0)),
                   pl.BlockSpec((tb, d), lambda i, t: (jnp.maximum(i - nct, 0), 0))),
        scratch_shapes=[pltpu.VMEM((d, tb), F32)] + [pltpu.VMEM((ec, tb), BF16) for _ in range(2 * n_sub)],
        compiler_params=_params("arbitrary", "arbitrary"),
        name="peer",
    )(h2, u, vt, e1, e2, th, gelu_consts, x1, mods3, npost.reshape(1, d))


def _grid_posemb(rows, d):
    r = np.repeat(np.arange(rows, dtype=np.float32), GRID_W)
    col = np.tile(np.arange(GRID_W, dtype=np.float32), rows)
    quarter = d // 4
    omega = (1.0 / (10000.0 ** (np.arange(quarter, dtype=np.float32) / np.float32(quarter)))).astype(np.float32)
    ar = r[:, None] * omega
    ac = col[:, None] * omega
    return np.concatenate([np.sin(ar), np.cos(ar), np.sin(ac), np.cos(ac)], axis=-1).astype(np.float32)


def kernel(x_prompt, x_sample, state_ssd, c, c_ctx, w_ada, b_ada, norm_mix_pre, norm_mix_post, norm_ffn_pre,
           norm_ffn_post, w_in, w_conv, b_conv, a_log, dt_bias, d_skip, ssd_norm, w_fourier, w_ssd_out, w_out,
           peer_query, peer_sub_keys, peer_u, peer_v):
    n_ctx_seq, l_ctx, d = x_prompt.shape
    n_lat, l_lat, _ = x_sample.shape
    depth = w_ada.shape[0]
    assert depth == 1
    n_ctx = n_ctx_seq * l_ctx
    span = l_lat
    assert span % l_ctx == 0 and n_ctx % span == 0 and n_ctx > 0 and n_lat > 0
    assert l_ctx & (l_ctx - 1) == 0 and span & (span - 1) == 0
    assert l_ctx % 256 == 0 and span % PEER_TOKEN_BLOCK == 0 and n_ctx % PEER_TOKEN_BLOCK == 0
    n_ctx_spans = n_ctx // span
    fd = FOURIER_GROUPS * FOURIER_GROUP_DIM
    d_inner = 2 * d
    n_heads = d_inner // HEAD_DIM
    assert n_heads == N_GROUPS * HEADS_PER_GROUP
    gn = N_GROUPS * D_STATE
    conv_dim = d_inner + 2 * gn
    col_z, col_xbc, col_dt = fd, fd + d_inner, fd + d_inner + conv_dim
    col_gate = col_dt + 2 * n_heads
    l = 0

    rows = n_lat + 1
    rows_pad = -(-rows // 8) * 8
    cond = jnp.concatenate([c, c_ctx[None, :], jnp.zeros((rows_pad - rows, d), F32)], axis=0)
    mods3 = _mods(cond, w_ada[l], b_ada[l]).reshape(rows_pad, 6, d)

    pos = jnp.asarray(_grid_posemb(l_lat // GRID_W, d))
    x0, hh, hl = _norm1(x_prompt.reshape(n_ctx, d), x_sample.reshape(n_lat * l_lat, d), pos, mods3,
                        norm_mix_pre[l], l_lat=l_lat, n_lat=n_lat)

    wi = w_in[l]
    w_main = jnp.concatenate([wi[:, col_z:col_xbc], wi[:, col_gate:], wi[:, :fd], wi[:, col_xbc:col_dt]],
                             axis=1).astype(BF16)
    proj = _mm(hh, w_main, BF16)
    z_cb, g_cb = 0, 1
    f_cb = (2 * d_inner) // fd
    xbc_col0 = 2 * d_inner + fd

    assert 2 * n_heads <= LANES
    wdt = jnp.pad(wi[:, col_dt:col_gate], ((0, 0), (0, LANES - 2 * n_heads)))
    wdt_h = wdt.astype(BF16)
    wdt_l = (wdt - wdt_h.astype(F32)).astype(BF16)
    dt_raw = _dt_proj(hh, hl, wdt_h, wdt_l)

    lanes = lambda p: jnp.pad(p.reshape(1, 2 * n_heads), ((0, 0), (0, LANES - 2 * n_heads)))
    par = jnp.concatenate([lanes(a_log[l]), lanes(dt_bias[l]), jnp.zeros((6, LANES), F32)], axis=0)
    dsk = jnp.repeat(d_skip[l], HEAD_DIM, axis=1)

    a_out = _fourier(proj, w_fourier[l].astype(BF16), span=span, l_ctx=l_ctx, n_ctx_spans=n_ctx_spans, col_block=f_cb)
    xbc = _conv(proj, w_conv[l], b_conv[l], span=span, l_ctx=l_ctx, n_ctx_spans=n_ctx_spans, col0=xbc_col0)
    h0 = state_ssd[:, l].reshape(n_lat, 2, N_GROUPS * PAIRS_PER_GROUP, 2 * HEAD_DIM, D_STATE)
    yf, yb, hfin = _ssd(xbc, dt_raw, par, dsk, h0, span=span, l_ctx=l_ctx, n_ctx_spans=n_ctx_spans,
                        n_ctx_seq=n_ctx_seq)

    x1, h2 = _merge(yf, yb, proj, a_out, x0, mods3, ssd_norm[l], w_ssd_out[l].astype(BF16), w_out[l].astype(BF16),
                    norm_mix_post[l], norm_ffn_pre[l], z_cb=z_cb, g_cb=g_cb, n_ctx=n_ctx, l_lat=l_lat, n_lat=n_lat)

    wq = jnp.transpose(peer_query[l].reshape(d, PEER_HEADS, 2 * N_KEYS), (1, 0, 2)).astype(BF16)
    e1, e2, th = _router(h2, wq, peer_sub_keys[l].astype(BF16))
    yc, ys = _peer(h2, peer_u[l].astype(BF16), peer_v[l].T.astype(BF16), e1, e2, th, x1, mods3, norm_ffn_post[l],
                   n_ctx=n_ctx, l_lat=l_lat, n_lat=n_lat)

    new_state = hfin.reshape(n_ctx_seq, 1, 2, n_heads, HEAD_DIM, D_STATE).astype(x_prompt.dtype)
    return (yc.reshape(n_ctx_seq, l_ctx, d), ys.reshape(n_lat, l_lat, d), new_state)
```

```python
import functools
import math

import numpy as np
import jax
import jax.numpy as jnp
from jax import lax
from jax.experimental import pallas as pl
from jax.experimental.pallas import tpu as pltpu

F32 = jnp.float32
BF16 = jnp.bfloat16

EPS = 1e-6
GRID_W = 64
FOURIER_GROUPS = 8
FOURIER_GROUP_DIM = 128
HEAD_DIM = 64
N_GROUPS = 4
HEADS_PER_GROUP = 8
D_STATE = 128
CONV_W = 5
CHUNK = 128
PEER_HEADS = 8
N_KEYS = 128
TOPK = 16

V7X_VMEM_BYTES = 64 * 1024 * 1024
VMEM_LIMIT_BYTES = V7X_VMEM_BYTES - 8 * 1024 * 1024
LANES = 128

_NT = (((1,), (1,)), ((), ()))


def _params(*sem, flags=None):
    return pltpu.CompilerParams(dimension_semantics=sem, vmem_limit_bytes=VMEM_LIMIT_BYTES, flags=flags)


def _pick(n, candidates):
    for c in candidates:
        if n % c == 0:
            return c
    raise ValueError(f"no tile in {candidates} divides {n}")


def _dot(a, b):
    return jnp.dot(a, b, preferred_element_type=F32)


def _dot_nt(a, b):
    return lax.dot_general(a, b, _NT, preferred_element_type=F32)


def _split3(x):
    x1 = x.astype(BF16)
    r1 = x - x1.astype(F32)
    x2 = r1.astype(BF16)
    x3 = (r1 - x2.astype(F32)).astype(BF16)
    return x1, x2, x3


def _sigmoid(x):
    return 1.0 / (1.0 + jnp.exp(-x))


def _silu(x):
    return x * _sigmoid(x)


def _rms(x, w):
    return x * lax.rsqrt(jnp.mean(x * x, axis=-1, keepdims=True) + EPS) * w


def _mods_kernel(c_ref, w_ref, b_ref, o_ref):
    s = _silu(c_ref[...])
    s1, s2, s3 = _split3(s)
    w1, w2, w3 = _split3(w_ref[...])
    acc = _dot(s1, w1) + (_dot(s1, w2) + _dot(s2, w1)) + (_dot(s1, w3) + _dot(s2, w2) + _dot(s3, w1))
    o_ref[...] = acc + b_ref[...]


def _mods(cond, w_ada, b_ada):
    rows, d = cond.shape
    n = w_ada.shape[1]
    tn = _pick(n, (1024, 512, 256, 128))
    return pl.pallas_call(
        _mods_kernel,
        out_shape=jax.ShapeDtypeStruct((rows, n), F32),
        grid=(n // tn,),
        in_specs=[pl.BlockSpec((rows, d), lambda j: (0, 0)),
                  pl.BlockSpec((d, tn), lambda j: (0, j)),
                  pl.BlockSpec((1, tn), lambda j: (0, j))],
        out_specs=pl.BlockSpec((rows, tn), lambda j: (0, j)),
        compiler_params=_params("arbitrary"),
        name="mods",
    )(cond, w_ada, b_ada.reshape(1, n))


def _mod_index(tile, n_ctx_tiles, tiles_per_lat, n_lat):
    t = jnp.maximum(tile - n_ctx_tiles, 0)
    return jnp.where(tile < n_ctx_tiles, n_lat, lax.div(t, tiles_per_lat))


def _norm1_kernel(xc_ref, xs_ref, pos_ref, mod_ref, w_ref, x0_ref, hh_ref, hl_ref, *, n_ctx_tiles):
    i = pl.program_id(0)
    x = jnp.where(i < n_ctx_tiles, xc_ref[...], xs_ref[...] + pos_ref[...])
    x0_ref[...] = x
    h = _rms(x, w_ref[...]) * (1.0 + mod_ref[0, 1:2, :]) + mod_ref[0, 0:1, :]
    hh = h.astype(BF16)
    hh_ref[...] = hh
    hl_ref[...] = (h - hh.astype(F32)).astype(BF16)


def _norm1(xc, xs, pos, mods3, w, *, l_lat, n_lat):
    n_ctx, d = xc.shape
    n_tok = n_ctx + xs.shape[0]
    tm = 256
    nct = n_ctx // tm
    tpl = l_lat // tm
    return pl.pallas_call(
        functools.partial(_norm1_kernel, n_ctx_tiles=nct),
        out_shape=(jax.ShapeDtypeStruct((n_tok, d), F32),
                   jax.ShapeDtypeStruct((n_tok, d), BF16),
                   jax.ShapeDtypeStruct((n_tok, d), BF16)),
        grid=(n_tok // tm,),
        in_specs=[pl.BlockSpec((tm, d), lambda i: (jnp.minimum(i, nct - 1), 0)),
                  pl.BlockSpec((tm, d), lambda i: (jnp.maximum(i - nct, 0), 0)),
                  pl.BlockSpec((tm, d), lambda i: (lax.rem(jnp.maximum(i - nct, 0), tpl), 0)),
                  pl.BlockSpec((1, 6, d), lambda i: (_mod_index(i, nct, tpl, n_lat), 0, 0)),
                  pl.BlockSpec((1, d), lambda i: (0, 0))],
        out_specs=(pl.BlockSpec((tm, d), lambda i: (i, 0)),
                   pl.BlockSpec((tm, d), lambda i: (i, 0)),
                   pl.BlockSpec((tm, d), lambda i: (i, 0))),
        compiler_params=_params("arbitrary"),
        name="norm1",
    )(xc, xs, pos, mods3, w.reshape(1, d))


def _mm_kernel(a_ref, b_ref, o_ref):
    o_ref[...] = _dot(a_ref[...], b_ref[...]).astype(o_ref.dtype)


def _mm(a, b, out_dtype):
    m, k = a.shape
    n = b.shape[1]
    tm = _pick(m, (1024, 512, 256))
    tn = _pick(n, (1024, 512, 256, 128))
    return pl.pallas_call(
        _mm_kernel,
        out_shape=jax.ShapeDtypeStruct((m, n), out_dtype),
        grid=(m // tm, n // tn),
        in_specs=[pl.BlockSpec((tm, k), lambda i, j: (i, 0)),
                  pl.BlockSpec((k, tn), lambda i, j: (0, j))],
        out_specs=pl.BlockSpec((tm, tn), lambda i, j: (i, j)),
        compiler_params=_params("arbitrary", "arbitrary"),
        name="mm",
    )(a, b)


def _dt_kernel(hh_ref, hl_ref, wh_ref, wl_ref, o_ref):
    hh = hh_ref[...]
    wh = wh_ref[...]
    o_ref[...] = _dot(hh, wh) + (_dot(hh, wl_ref[...]) + _dot(hl_ref[...], wh))


def _dt_proj(hh, hl, wh, wl):
    m, k = hh.shape
    n = wh.shape[1]
    tm = _pick(m, (1024, 512, 256))
    return pl.pallas_call(
        _dt_kernel,
        out_shape=jax.ShapeDtypeStruct((m, n), F32),
        grid=(m // tm,),
        in_specs=[pl.BlockSpec((tm, k), lambda i: (i, 0)),
                  pl.BlockSpec((tm, k), lambda i: (i, 0)),
                  pl.BlockSpec((k, n), lambda i: (0, 0)),
                  pl.BlockSpec((k, n), lambda i: (0, 0))],
        out_specs=pl.BlockSpec((tm, n), lambda i: (i, 0)),
        compiler_params=_params("arbitrary"),
        name="dt_proj",
    )(hh, hl, wh, wl)


def _dft_mats(n):
    idx = np.arange(n, dtype=np.int64)
    ang = 2.0 * np.pi * ((idx[:, None] * idx[None, :]) % n).astype(np.float64) / n
    scale = 1.0 / math.sqrt(n)
    return np.cos(ang) * scale, np.sin(ang) * scale


def _span_dft(span, seq_len):
    c, s = _dft_mats(seq_len)
    eye = np.eye(span // seq_len)
    return np.kron(eye, c), np.kron(eye, s)


def _fourier_kernel(u_ref, cs_ref, cl_ref, sl_ref, wf_ref, o_ref, csb_ref, clb_ref, slb_ref, *, n_ctx_spans):
    s = pl.program_id(0)

    @pl.when(jnp.logical_or(s == 0, s == n_ctx_spans))
    def _():
        csb_ref[...] = cs_ref[...].astype(BF16)
        clb_ref[...] = cl_ref[0].astype(BF16)
        slb_ref[...] = sl_ref[0].astype(BF16)

    u = u_ref[...]
    pc, ps = [], []
    for g in range(FOURIER_GROUPS):
        p = _dot(u[:, g * FOURIER_GROUP_DIM:(g + 1) * FOURIER_GROUP_DIM], csb_ref[...])
        pc.append(p[:, :FOURIER_GROUP_DIM])
        ps.append(p[:, FOURIER_GROUP_DIM:])
    pc = jnp.concatenate(pc, axis=1).astype(BF16)
    ps = jnp.concatenate(ps, axis=1).astype(BF16)
    f = _dot(clb_ref[...], pc) - _dot(slb_ref[...], ps)
    o_ref[...] = _dot(f.astype(BF16), wf_ref[...]).astype(o_ref.dtype)


def _fourier(proj, w_fourier, *, span, l_ctx, n_ctx_spans, col_block):
    n_tok = proj.shape[0]
    fd = FOURIER_GROUPS * FOURIER_GROUP_DIM
    d = w_fourier.shape[1]
    cc, sc = _dft_mats(FOURIER_GROUP_DIM)
    cs = np.concatenate([cc, sc], axis=1)
    c_ctx, s_ctx = _span_dft(span, l_ctx)
    c_lat, s_lat = _dft_mats(span)
    cl = jnp.asarray(np.stack([c_ctx, c_lat]), F32)
    sl = jnp.asarray(np.stack([s_ctx, s_lat]), F32)
    kind = lambda s: (jnp.where(s < n_ctx_spans, 0, 1), 0, 0)
    return pl.pallas_call(
        functools.partial(_fourier_kernel, n_ctx_spans=n_ctx_spans),
        out_shape=jax.ShapeDtypeStruct((n_tok, d), BF16),
        grid=(n_tok // span,),
        in_specs=[pl.BlockSpec((span, fd), lambda s: (s, col_block)),
                  pl.BlockSpec(cs.shape, lambda s: (0, 0)),
                  pl.BlockSpec((1, span, span), kind),
                  pl.BlockSpec((1, span, span), kind),
                  pl.BlockSpec((fd, d), lambda s: (0, 0))],
        out_specs=pl.BlockSpec((span, d), lambda s: (s, 0)),
        scratch_shapes=[pltpu.VMEM(cs.shape, BF16), pltpu.VMEM((span, span), BF16), pltpu.VMEM((span, span), BF16)],
        compiler_params=_params("arbitrary"),
        name="fourier",
    )(proj, jnp.asarray(cs, F32), cl, sl, w_fourier)


def _conv_kernel(u_ref, w_ref, b_ref, o_ref, *, n_ctx_spans, l_ctx):
    n = u_ref.shape[0]
    half = CONV_W // 2
    seq_len = jnp.where(pl.program_id(0) < n_ctx_spans, l_ctx, n)

    def taps(u, pos=None):
        rows = u.shape[0]
        acc = b_ref[...] + u * w_ref[half:half + 1, :]
        for k in range(CONV_W):
            off = k - half
            if off == 0:
                continue
            shifted = pltpu.roll(u, (-off) % rows, axis=0)
            if pos is not None:
                shifted = jnp.where(jnp.logical_and(pos + off >= 0, pos + off < seq_len), shifted, 0.0)
            acc = acc + shifted * w_ref[k:k + 1, :]
        return _silu(acc).astype(o_ref.dtype)

    o_ref[...] = taps(u_ref[...].astype(F32))
    slab, fix = 4 * SUBLANES, 2 * SUBLANES
    for b in range(0, n + 1, l_ctx):
        lo = min(max(b - slab // 2, 0), n - slab)
        w0 = max(b - fix // 2, 0)
        w1 = min(b + fix // 2, n)
        pos = jnp.bitwise_and(lo + lax.broadcasted_iota(jnp.int32, (slab, u_ref.shape[1]), 0), seq_len - 1)
        redo = taps(u_ref[lo:lo + slab, :].astype(F32), pos)
        o_ref[w0:w1, :] = redo[w0 - lo:w1 - lo, :]


def _conv(proj, w_conv, b_conv, *, span, l_ctx, n_ctx_spans, col0):
    n_tok = proj.shape[0]
    cdim = w_conv.shape[1]
    tc = 512
    cb0 = col0 // tc
    return pl.pallas_call(
        functools.partial(_conv_kernel, n_ctx_spans=n_ctx_spans, l_ctx=l_ctx),
        out_shape=jax.ShapeDtypeStruct((n_tok, cdim), BF16),
        grid=(n_tok // span, cdim // tc),
        in_specs=[pl.BlockSpec((span, tc), lambda s, j: (s, cb0 + j)),
                  pl.BlockSpec((CONV_W, tc), lambda s, j: (0, j)),
                  pl.BlockSpec((1, tc), lambda s, j: (0, j))],
        out_specs=pl.BlockSpec((span, tc), lambda s, j: (s, j)),
        compiler_params=_params("arbitrary", "arbitrary"),
        name="conv",
    )(proj, w_conv, b_conv.reshape(1, cdim))


PAIRS_PER_GROUP = HEADS_PER_GROUP // 2


def _ssd_kernel(xf_ref, bf_ref, cf_ref, dtf_ref, xb_ref, bb_ref, cb_ref, dtb_ref, par_ref, dsk_ref, h0_ref,
                yf_ref, yb_ref, hf_ref, st_ref, *, n_ctx_spans, chunks_per_ctx_seq, n_chunks):
    s = pl.program_id(0)
    c = pl.program_id(2)
    is_ctx = s < n_ctx_spans
    cps = chunks_per_ctx_seq
    seq_start = jnp.where(is_ctx, lax.rem(c, cps) == 0, c == 0)
    seq_end = jnp.logical_and(is_ctx, lax.rem(c, cps) == cps - 1)

    @pl.when(jnp.logical_and(seq_start, is_ctx))
    def _():
        st_ref[...] = jnp.zeros(st_ref.shape, F32)

    @pl.when(jnp.logical_and(seq_start, jnp.logical_not(is_ctx)))
    def _():
        st_ref[...] = h0_ref[0]

    li = lax.broadcasted_iota(jnp.int32, (CHUNK, CHUNK), 0)
    si = lax.broadcasted_iota(jnp.int32, (CHUNK, CHUNK), 1)
    lane_lo = si < HEAD_DIM
    row_lo = li < HEAD_DIM
    gw = HEADS_PER_GROUP * HEAD_DIM
    n_groups = N_GROUPS

    for d, (x_ref, b_ref, c_ref, dt_ref, y_ref) in enumerate(((xf_ref, bf_ref, cf_ref, dtf_ref, yf_ref),
                                                                 (xb_ref, bb_ref, cb_ref, dtb_ref, yb_ref))):
        mask = (si <= li) if d == 0 else (si >= li)
        mb = mask.astype(BF16)
        raw = dt_ref[...] + par_ref[1:2, :]
        dt = jnp.maximum(raw, 0.0) + jnp.log1p(jnp.exp(-jnp.abs(raw)))
        da = dt * (-jnp.exp(par_ref[0:1, :]))
        d1, d2, d3 = _split3(da)
        cs = _dot(mb, d1) + _dot(mb, d2) + _dot(mb, d3)
        tot = cs[CHUNK - 1:CHUNK, :] if d == 0 else cs[0:1, :]
        w_t = (dt * jnp.exp(tot - cs)).T
        cs_t = cs.T
        dt_t = dt.T
        ecs = jnp.exp(cs)
        etot_t = jnp.exp(cs_t[:, CHUNK - 1:CHUNK] if d == 0 else cs_t[:, 0:1])
        for gi in range(n_groups):
            gl = slice(gi * LANES, (gi + 1) * LANES)
            bm = b_ref[:, gl]
            cm = c_ref[:, gl]
            cb = _dot_nt(cm, bm)
            x = x_ref[:, gi * gw:(gi + 1) * gw]
            x_t = x.astype(F32).T
            for kp in range(PAIRS_PER_GROUP):
                l0 = (d * n_groups + gi) * HEADS_PER_GROUP + 2 * kp
                l1 = l0 + 1
                ps = slice(kp * LANES, (kp + 1) * LANES)
                ys = slice(gi * gw + kp * LANES, gi * gw + (kp + 1) * LANES)
                pair = gi * PAIRS_PER_GROUP + kp
                xp = x[:, ps]
                lms = []
                for ln in (l0, l1):
                    seg = cs[:, ln:ln + 1] - cs_t[ln:ln + 1, :]
                    dec = jnp.exp(jnp.where(mask, seg, -jnp.inf))
                    lms.append((cb * dec * dt_t[ln:ln + 1, :]).astype(BF16))
                hp = st_ref[d, pair]
                y = jnp.where(lane_lo, _dot(lms[0], xp), _dot(lms[1], xp))
                y = y + _dot_nt(cm, hp.astype(BF16)) * jnp.where(lane_lo, ecs[:, l0:l0 + 1], ecs[:, l1:l1 + 1])
                y = y + dsk_ref[d:d + 1, ys] * xp.astype(F32)
                y_ref[:, ys] = y.astype(y_ref.dtype)
                xw = jnp.where(row_lo, w_t[l0:l0 + 1, :], w_t[l1:l1 + 1, :]) * x_t[ps, :]
                keep = jnp.where(row_lo, etot_t[l0:l0 + 1, :], etot_t[l1:l1 + 1, :])
                st_ref[d, pair] = hp * keep + _dot(xw.astype(BF16), bm)

    @pl.when(seq_end)
    def _():
        hf_ref[lax.div(c, cps), 0] = st_ref[0]
        hf_ref[lax.div(n_chunks - 1 - c, cps), 1] = st_ref[1]


def _ssd(xbc, dt, par, dsk, h0, *, span, l_ctx, n_ctx_spans, n_ctx_seq):
    n_tok = xbc.shape[0]
    d_inner = N_GROUPS * HEADS_PER_GROUP * HEAD_DIM
    nc = span // CHUNK
    cps = l_ctx // CHUNK
    spc = span // l_ctx
    gps = N_GROUPS
    n_gsteps = N_GROUPS // gps
    gw = HEADS_PER_GROUP * HEAD_DIM * gps
    sw = D_STATE * gps
    b_cb = d_inner // sw
    c_cb = b_cb + n_gsteps
    st_block = (1, PAIRS_PER_GROUP * gps, 2 * HEAD_DIM, D_STATE)

    rf = lambda s, g, c: s * nc + c
    rb = lambda s, g, c: s * nc + (nc - 1 - c)

    hf_block = (spc, 2) + st_block[1:]
    hf_index = lambda s, g, c: (jnp.minimum(s, n_ctx_spans - 1), 0, 0, 0, 0)

    in_specs = []
    for r in (rf, rb):
        in_specs += [pl.BlockSpec((CHUNK, gw), lambda s, g, c, r=r: (r(s, g, c), g)),
                     pl.BlockSpec((CHUNK, sw), lambda s, g, c, r=r: (r(s, g, c), b_cb + g)),
                     pl.BlockSpec((CHUNK, sw), lambda s, g, c, r=r: (r(s, g, c), c_cb + g)),
                     pl.BlockSpec((CHUNK, LANES), lambda s, g, c, r=r: (r(s, g, c), 0))]
    in_specs += [pl.BlockSpec((8, LANES), lambda s, g, c: (0, 0)),
                 pl.BlockSpec((2, gw), lambda s, g, c: (0, g)),
                 pl.BlockSpec((1, 2) + st_block[1:], lambda s, g, c: (jnp.maximum(s - n_ctx_spans, 0), 0, g, 0, 0))]
    hf_shape = jax.ShapeDtypeStruct((n_ctx_seq, 2, N_GROUPS * PAIRS_PER_GROUP, 2 * HEAD_DIM, D_STATE), F32)
    return pl.pallas_call(
        functools.partial(_ssd_kernel, n_ctx_spans=n_ctx_spans, chunks_per_ctx_seq=cps, n_chunks=nc),
        out_shape=(jax.ShapeDtypeStruct((n_tok, d_inner), BF16), jax.ShapeDtypeStruct((n_tok, d_inner), BF16),
                   hf_shape),
        grid=(n_tok // span, n_gsteps, nc),
        in_specs=in_specs,
        out_specs=(pl.BlockSpec((CHUNK, gw), lambda s, g, c: (rf(s, g, c), g)),
                   pl.BlockSpec((CHUNK, gw), lambda s, g, c: (rb(s, g, c), g)),
                   pl.BlockSpec(hf_block, hf_index)),
        scratch_shapes=[pltpu.VMEM((2,) + st_block[1:], F32)],
        compiler_params=_params("arbitrary", "arbitrary", "arbitrary"),
        name="ssd",
    )(xbc, xbc, xbc, dt, xbc, xbc, xbc, dt, par, dsk, h0)


def _merge_kernel(yf_ref, yb_ref, z_ref, g_ref, a_ref, x0_ref, mod_ref, nssd_ref, wso_ref, wo_ref,
                  npost_ref, nffn_ref, x1_ref, h2_ref):
    d = x0_ref.shape[1]
    z = z_ref[...].astype(F32)
    y = (yf_ref[...].astype(F32) + yb_ref[...].astype(F32)) * _silu(z)
    y = _rms(y, nssd_ref[...])
    b_out = _dot(y.astype(BF16), wso_ref[...])
    g = g_ref[...].astype(F32)
    merged = _sigmoid(g[:, :d]) * a_ref[...].astype(F32) + _sigmoid(g[:, d:]) * b_out
    mix = _dot(merged.astype(BF16), wo_ref[...])
    x1 = x0_ref[...] + mod_ref[0, 2:3, :] * _rms(mix, npost_ref[...])
    x1_ref[...] = x1
    h2 = _rms(x1, nffn_ref[...]) * (1.0 + mod_ref[0, 4:5, :]) + mod_ref[0, 3:4, :]
    h2_ref[...] = h2.astype(BF16)


def _merge(yf, yb, proj, a_out, x0, mods3, ssd_norm, wso, wo, npost, nffn, *, z_cb, g_cb, n_ctx, l_lat, n_lat):
    n_tok, d = x0.shape
    d_inner = yf.shape[1]
    tm = 256
    nct, tpl = n_ctx // tm, l_lat // tm
    row = lambda i: (i, 0)
    const = lambda i: (0, 0)
    return pl.pallas_call(
        _merge_kernel,
        out_shape=(jax.ShapeDtypeStruct((n_tok, d), F32), jax.ShapeDtypeStruct((n_tok, d), BF16)),
        grid=(n_tok // tm,),
        in_specs=[pl.BlockSpec((tm, d_inner), row),
                  pl.BlockSpec((tm, d_inner), row),
                  pl.BlockSpec((tm, d_inner), lambda i: (i, z_cb)),
                  pl.BlockSpec((tm, 2 * d), lambda i: (i, g_cb)),
                  pl.BlockSpec((tm, d), row),
                  pl.BlockSpec((tm, d), row),
                  pl.BlockSpec((1, 6, d), lambda i: (_mod_index(i, nct, tpl, n_lat), 0, 0)),
                  pl.BlockSpec((1, d_inner), const),
                  pl.BlockSpec((d_inner, d), const),
                  pl.BlockSpec((d, d), const),
                  pl.BlockSpec((1, d), const),
                  pl.BlockSpec((1, d), const)],
        out_specs=(pl.BlockSpec((tm, d), row), pl.BlockSpec((tm, d), row)),
        compiler_params=_params("arbitrary"),
        name="merge",
    )(yf, yb, proj, proj, a_out, x0, mods3, ssd_norm.reshape(1, d_inner), wso, wo,
      npost.reshape(1, d), nffn.reshape(1, d))


SUBLANES = 8
ROUTER_TOKEN_BLOCK = SUBLANES * LANES


def _sort_pairs(n):
    pairs = []
    p = 1
    while p < n:
        k = p
        while k >= 1:
            for j in range(k % p, n - k, 2 * k):
                for i in range(min(k, n - j - k)):
                    if (i + j) // (2 * p) == (i + j + k) // (2 * p):
                        pairs.append((i + j, i + j + k))
            k //= 2
        p *= 2
    return pairs


def _sort_desc(v):
    v = list(v)
    for a, b in _sort_pairs(len(v)):
        v[a], v[b] = jnp.maximum(v[a], v[b]), jnp.minimum(v[a], v[b])
    return v


def _merge_top(a, b):
    n = len(a)
    c = [jnp.maximum(a[i], b[n - 1 - i]) for i in range(n)]
    s = n // 2
    while s >= 1:
        for i in range(n):
            if i & s == 0:
                c[i], c[i + s] = jnp.maximum(c[i], c[i + s]), jnp.minimum(c[i], c[i + s])
        s //= 2
    return c


def _top_sorted(items, keep=TOPK):
    groups = [_sort_desc(items[g:g + keep]) for g in range(0, len(items), keep)]
    while len(groups) > 1:
        groups = [_merge_top(groups[i], groups[i + 1]) for i in range(0, len(groups), 2)]
    return groups[0]


_STAIRCASE = [(k, l) for k in range(TOPK) for l in range(TOPK) if (k + 1) * (l + 1) <= TOPK]
_STAIRCASE_PAD = 4 * TOPK - len(_STAIRCASE)


ROUTER_ROW_BLOCK = 16


def _router_kernel(h2_ref, wq_ref, keys_ref, e1_ref, need_ref, e2_ref, code_ref, s_ref, k8_ref):
    def to_row(v8):
        return jnp.concatenate([v8[r:r + 1, :] for r in range(SUBLANES)], axis=1)

    def head(h, carry):
        q = _dot(h2_ref[...], wq_ref[h])
        tops = []
        for half in range(2):
            s_t = _dot_nt(keys_ref[h, half], q[:, half * N_KEYS:(half + 1) * N_KEYS].astype(BF16))
            s_ref[half] = s_t
            for r in range(SUBLANES):
                k8_ref[half, pl.ds(r, N_KEYS, stride=SUBLANES), :] = s_t[:, r * LANES:(r + 1) * LANES]
            tops.append(_top_sorted([k8_ref[half, k * SUBLANES:(k + 1) * SUBLANES, :] for k in range(N_KEYS)]))
        a = [jnp.exp(t - tops[0][0]) for t in tops[0]]
        b = [jnp.exp(t - tops[1][0]) for t in tops[1]]
        cand = [a[k] * b[l] for k, l in _STAIRCASE]
        best = _top_sorted(cand + [jnp.full_like(cand[0], -1.0)] * _STAIRCASE_PAD)
        theta = best[-1]
        inv_z = 1.0 / functools.reduce(lambda u, v: u + v, best)
        counts = [None] * TOPK
        for (k, l), c in zip(_STAIRCASE, cand):
            hit = jnp.where(c >= theta, 1.0, 0.0)
            counts[k] = hit if counts[k] is None else counts[k] + hit
        a_rows = [to_row(t) for t in tops[0]]
        b_rows = [to_row(t) for t in tops[1]]
        count_rows = [to_row(c) for c in counts]
        m1, m2, inv_z_row = a_rows[0], b_rows[0], to_row(inv_z)
        for rb in range(N_KEYS // ROUTER_ROW_BLOCK):
            rs = slice(rb * ROUTER_ROW_BLOCK, (rb + 1) * ROUTER_ROW_BLOCK)
            s1 = s_ref[0, rs, :]
            s2 = s_ref[1, rs, :]
            code = None
            for l in range(TOPK):
                hit = jnp.where(s2 >= b_rows[l], 1.0, 0.0)
                code = hit if code is None else code + hit
            take = None
            for k in range(TOPK):
                v = jnp.where(s1 >= a_rows[k], count_rows[k], 0.0)
                take = v if take is None else jnp.maximum(take, v)
            e1_ref[h, rs, :] = jnp.exp(s1 - m1) * inv_z_row
            need_ref[h, rs, :] = (TOPK + 1.0) - take
            e2_ref[h, rs, :] = jnp.exp(s2 - m2).astype(BF16)
            code_ref[h, rs, :] = code.astype(BF16)
        return carry

    lax.fori_loop(0, PEER_HEADS, head, 0)


def _router(h2, wq, keys):
    n_tok, d = h2.shape
    tb = ROUTER_TOKEN_BLOCK
    return pl.pallas_call(
        _router_kernel,
        out_shape=tuple(jax.ShapeDtypeStruct((PEER_HEADS, N_KEYS, n_tok), dt) for dt in (F32, F32, BF16, BF16)),
        grid=(n_tok // tb,),
        in_specs=[pl.BlockSpec((tb, d), lambda i: (i, 0)),
                  pl.BlockSpec(wq.shape, lambda i: (0, 0, 0)),
                  pl.BlockSpec(keys.shape, lambda i: (0, 0, 0, 0))],
        out_specs=tuple(pl.BlockSpec((PEER_HEADS, N_KEYS, tb), lambda i: (0, 0, i)) for _ in range(4)),
        scratch_shapes=[pltpu.VMEM((2, N_KEYS, tb), F32), pltpu.VMEM((2, N_KEYS * SUBLANES, LANES), F32)],
        compiler_params=_params("arbitrary"),
        name="router",
    )(h2, wq, keys)


PEER_TOKEN_BLOCK = 512
PEER_EXPERT_CHUNK = 512
PEER_CHUNKS_PER_STEP = 4


BF16_TILE_ROWS = 16
GELU_C = math.sqrt(2.0 / math.pi)
GELU_K = 0.044715


def _gelu_tanh(x, c, k):
    return 0.5 * x * (1.0 + jnp.tanh(c * (x + k * (x * x * x))))


def _peer_weighted_act(e1_ref, need_ref, i0, e2_ref, code_ref, gc_ref, at_ref, gw_ref):
    e_chunk, tb = at_ref.shape
    gc = gc_ref[0:1, :].astype(at_ref.dtype)
    gk = gc_ref[1:2, :].astype(at_ref.dtype)
    for ii in range(e_chunk // N_KEYS):
        rs = slice(ii * N_KEYS, (ii + 1) * N_KEYS)
        row = slice(i0 + ii, i0 + ii + 1)
        tile = (N_KEYS // BF16_TILE_ROWS, BF16_TILE_ROWS, LANES)
        for lb in range(tb // LANES):
            ls = slice(lb * LANES, (lb + 1) * LANES)
            w = None
            for h in range(PEER_HEADS):
                need = jnp.broadcast_to(need_ref[h, row, ls].astype(BF16), tile[1:])[None]
                e1 = jnp.broadcast_to(e1_ref[h, row, ls].astype(BF16), tile[1:])[None]
                hs = slice(h * N_KEYS, (h + 1) * N_KEYS)
                pick = code_ref[hs, ls].reshape(tile) >= need
                sel = jnp.where(pick, e2_ref[hs, ls].reshape(tile) * e1, 0.0)
                w = sel if w is None else w + sel
            gw_ref[rs, ls] = w.reshape(N_KEYS, LANES) * _gelu_tanh(at_ref[rs, ls], gc, gk)


def _peer_kernel(h2_ref, u_ref, vt_ref, e1_ref, need_ref, e2_ref, code_ref, gc_ref, x1_ref, mod_ref, npost_ref,
                 yc_ref, ys_ref, acc_ref, e2s_ref, codes_ref, *bufs, n_it, n_ctx_blocks):
    it = pl.program_id(1)
    n_sub = PEER_CHUNKS_PER_STEP
    at_refs, gw_refs = bufs[:n_sub], bufs[n_sub:]
    ec = at_refs[0].shape[0]

    @pl.when(it == 0)
    def _():
        acc_ref[...] = jnp.zeros(acc_ref.shape, F32)
        for h in range(PEER_HEADS):
            e2s_ref[h * N_KEYS:(h + 1) * N_KEYS, :] = e2_ref[h]
            codes_ref[h * N_KEYS:(h + 1) * N_KEYS, :] = code_ref[h]

    h2 = h2_ref[...]

    def project(k):
        at_refs[k][...] = _dot_nt(u_ref[k * ec:(k + 1) * ec, :], h2).astype(BF16)

    def weigh(k):
        _peer_weighted_act(e1_ref, need_ref, k * (ec // N_KEYS), e2s_ref, codes_ref, gc_ref, at_refs[k], gw_refs[k])

    def combine(k):
        acc_ref[...] += _dot(vt_ref[:, k * ec:(k + 1) * ec], gw_refs[k][...])

    project(0)
    for k in range(n_sub):
        if k + 1 < n_sub:
            project(k + 1)
        weigh(k)
        combine(k)

    last = it == n_it - 1
    is_ctx = pl.program_id(0) < n_ctx_blocks

    def result():
        o = acc_ref[...].T
        return x1_ref[...] + mod_ref[0, 5:6, :] * _rms(o, npost_ref[...])

    @pl.when(jnp.logical_and(last, is_ctx))
    def _():
        yc_ref[...] = result()

    @pl.when(jnp.logical_and(last, jnp.logical_not(is_ctx)))
    def _():
        ys_ref[...] = result()


def _peer(h2, u, vt, e1, need, e2, code, x1, mods3, npost, *, n_ctx, l_lat, n_lat):
    n_tok, d = h2.shape
    n_exp = u.shape[0]
    tb, ec, n_sub = PEER_TOKEN_BLOCK, PEER_EXPERT_CHUNK, PEER_CHUNKS_PER_STEP
    step_e = ec * n_sub
    n_it = n_exp // step_e
    rows_e1 = step_e // N_KEYS
    nct, tpl = n_ctx // tb, l_lat // tb
    gelu_consts = jnp.asarray(np.repeat(np.array([[GELU_C], [GELU_K]] + [[0.0]] * 6, np.float32), LANES, axis=1))
    return pl.pallas_call(
        functools.partial(_peer_kernel, n_it=n_it, n_ctx_blocks=nct),
        out_shape=(jax.ShapeDtypeStruct((n_ctx, d), F32), jax.ShapeDtypeStruct((n_tok - n_ctx, d), F32)),
        grid=(n_tok // tb, n_it),
        in_specs=[pl.BlockSpec((tb, d), lambda i, t: (i, 0)),
                  pl.BlockSpec((step_e, d), lambda i, t: (t, 0)),
                  pl.BlockSpec((d, step_e), lambda i, t: (0, t)),
                  pl.BlockSpec((PEER_HEADS, rows_e1, tb), lambda i, t: (0, t, i)),
                  pl.BlockSpec((PEER_HEADS, rows_e1, tb), lambda i, t: (0, t, i)),
                  pl.BlockSpec((PEER_HEADS, N_KEYS, tb), lambda i, t: (0, 0, i)),
                  pl.BlockSpec((PEER_HEADS, N_KEYS, tb), lambda i, t: (0, 0, i)),
                  pl.BlockSpec(gelu_consts.shape, lambda i, t: (0, 0)),
                  pl.BlockSpec((tb, d), lambda i, t: (i, 0)),
                  pl.BlockSpec((1, 6, d), lambda i, t: (_mod_index(i, nct, tpl, n_lat), 0, 0)),
                  pl.BlockSpec((1, d), lambda i, t: (0, 0))],
        out_specs=(pl.BlockSpec((tb, d), lambda i, t: (jnp.minimum(i, nct - 1), 0)),
                   pl.BlockSpec((tb, d), lambda i, t: (jnp.maximum(i - nct, 0), 0))),
        scratch_shapes=[pltpu.VMEM((d, tb), F32)] + [pltpu.VMEM((PEER_HEADS * N_KEYS, tb), BF16) for _ in range(2)]
        + [pltpu.VMEM((ec, tb), BF16) for _ in range(2 * n_sub)],
        compiler_params=_params("arbitrary", "arbitrary"),
        name="peer",
    )(h2, u, vt, e1, need, e2, code, gelu_consts, x1, mods3, npost.reshape(1, d))


def _grid_posemb(rows, d):
    r = np.repeat(np.arange(rows, dtype=np.float32), GRID_W)
    col = np.tile(np.arange(GRID_W, dtype=np.float32), rows)
    quarter = d // 4
    omega = (1.0 / (10000.0 ** (np.arange(quarter, dtype=np.float32) / np.float32(quarter)))).astype(np.float32)
    ar = r[:, None] * omega
    ac = col[:, None] * omega
    return np.concatenate([np.sin(ar), np.cos(ar), np.sin(ac), np.cos(ac)], axis=-1).astype(np.float32)


def kernel(x_prompt, x_sample, state_ssd, c, c_ctx, w_ada, b_ada, norm_mix_pre, norm_mix_post, norm_ffn_pre,
           norm_ffn_post, w_in, w_conv, b_conv, a_log, dt_bias, d_skip, ssd_norm, w_fourier, w_ssd_out, w_out,
           peer_query, peer_sub_keys, peer_u, peer_v):
    n_ctx_seq, l_ctx, d = x_prompt.shape
    n_lat, l_lat, _ = x_sample.shape
    depth = w_ada.shape[0]
    assert depth == 1
    n_ctx = n_ctx_seq * l_ctx
    span = l_lat
    assert span % l_ctx == 0 and n_ctx % span == 0 and n_ctx > 0 and n_lat > 0
    assert l_ctx & (l_ctx - 1) == 0 and span & (span - 1) == 0
    assert l_ctx % 256 == 0 and span % PEER_TOKEN_BLOCK == 0 and n_ctx % PEER_TOKEN_BLOCK == 0
    n_ctx_spans = n_ctx // span
    fd = FOURIER_GROUPS * FOURIER_GROUP_DIM
    d_inner = 2 * d
    n_heads = d_inner // HEAD_DIM
    assert n_heads == N_GROUPS * HEADS_PER_GROUP
    gn = N_GROUPS * D_STATE
    conv_dim = d_inner + 2 * gn
    col_z, col_xbc, col_dt = fd, fd + d_inner, fd + d_inner + conv_dim
    col_gate = col_dt + 2 * n_heads
    l = 0

    rows = n_lat + 1
    rows_pad = -(-rows // 8) * 8
    cond = jnp.concatenate([c, c_ctx[None, :], jnp.zeros((rows_pad - rows, d), F32)], axis=0)
    mods3 = _mods(cond, w_ada[l], b_ada[l]).reshape(rows_pad, 6, d)

    pos = jnp.asarray(_grid_posemb(l_lat // GRID_W, d))
    x0, hh, hl = _norm1(x_prompt.reshape(n_ctx, d), x_sample.reshape(n_lat * l_lat, d), pos, mods3,
                        norm_mix_pre[l], l_lat=l_lat, n_lat=n_lat)

    wi = w_in[l]
    w_main = jnp.concatenate([wi[:, col_z:col_xbc], wi[:, col_gate:], wi[:, :fd], wi[:, col_xbc:col_dt]],
                             axis=1).astype(BF16)
    proj = _mm(hh, w_main, BF16)
    z_cb, g_cb = 0, 1
    f_cb = (2 * d_inner) // fd
    xbc_col0 = 2 * d_inner + fd

    assert 2 * n_heads <= LANES
    wdt = jnp.pad(wi[:, col_dt:col_gate], ((0, 0), (0, LANES - 2 * n_heads)))
    wdt_h = wdt.astype(BF16)
    wdt_l = (wdt - wdt_h.astype(F32)).astype(BF16)
    dt_raw = _dt_proj(hh, hl, wdt_h, wdt_l)

    lanes = lambda p: jnp.pad(p.reshape(1, 2 * n_heads), ((0, 0), (0, LANES - 2 * n_heads)))
    par = jnp.concatenate([lanes(a_log[l]), lanes(dt_bias[l]), jnp.zeros((6, LANES), F32)], axis=0)
    dsk = jnp.repeat(d_skip[l], HEAD_DIM, axis=1)

    a_out = _fourier(proj, w_fourier[l].astype(BF16), span=span, l_ctx=l_ctx, n_ctx_spans=n_ctx_spans, col_block=f_cb)
    xbc = _conv(proj, w_conv[l], b_conv[l], span=span, l_ctx=l_ctx, n_ctx_spans=n_ctx_spans, col0=xbc_col0)
    h0 = state_ssd[:, l].reshape(n_lat, 2, N_GROUPS * PAIRS_PER_GROUP, 2 * HEAD_DIM, D_STATE)
    yf, yb, hfin = _ssd(xbc, dt_raw, par, dsk, h0, span=span, l_ctx=l_ctx, n_ctx_spans=n_ctx_spans,
                        n_ctx_seq=n_ctx_seq)

    x1, h2 = _merge(yf, yb, proj, a_out, x0, mods3, ssd_norm[l], w_ssd_out[l].astype(BF16), w_out[l].astype(BF16),
                    norm_mix_post[l], norm_ffn_pre[l], z_cb=z_cb, g_cb=g_cb, n_ctx=n_ctx, l_lat=l_lat, n_lat=n_lat)

    wq = jnp.transpose(peer_query[l].reshape(d, PEER_HEADS, 2 * N_KEYS), (1, 0, 2)).astype(BF16)
    e1, need, e2, code = _router(h2, wq, peer_sub_keys[l].astype(BF16))
    yc, ys = _peer(h2, peer_u[l].astype(BF16), peer_v[l].T.astype(BF16), e1, need, e2, code, x1, mods3, norm_ffn_post[l],
                   n_ctx=n_ctx, l_lat=l_lat, n_lat=n_lat)

    new_state = hfin.reshape(n_ctx_seq, 1, 2, n_heads, HEAD_DIM, D_STATE).astype(x_prompt.dtype)
    return (yc.reshape(n_ctx_seq, l_ctx, d), ys.reshape(n_lat, l_lat, d), new_state)
```

```python
import functools
import math

import numpy as np
import jax
import jax.numpy as jnp
from jax import lax
from jax.experimental import pallas as pl
from jax.experimental.pallas import tpu as pltpu

F32 = jnp.float32
BF16 = jnp.bfloat16

EPS = 1e-6
GRID_W = 64
FOURIER_GROUPS = 8
FOURIER_GROUP_DIM = 128
HEAD_DIM = 64
N_GROUPS = 4
HEADS_PER_GROUP = 8
D_STATE = 128
CONV_W = 5
CHUNK = 128
PEER_HEADS = 8
N_KEYS = 128
TOPK = 16

V7X_VMEM_BYTES = 64 * 1024 * 1024
VMEM_LIMIT_BYTES = V7X_VMEM_BYTES - 8 * 1024 * 1024
LANES = 128

_NT = (((1,), (1,)), ((), ()))


def _params(*sem, flags=None):
    return pltpu.CompilerParams(dimension_semantics=sem, vmem_limit_bytes=VMEM_LIMIT_BYTES, flags=flags)


def _pick(n, candidates):
    for c in candidates:
        if n % c == 0:
            return c
    raise ValueError(f"no tile in {candidates} divides {n}")


def _dot(a, b):
    return jnp.dot(a, b, preferred_element_type=F32)


def _dot_nt(a, b):
    return lax.dot_general(a, b, _NT, preferred_element_type=F32)


def _split3(x):
    x1 = x.astype(BF16)
    r1 = x - x1.astype(F32)
    x2 = r1.astype(BF16)
    x3 = (r1 - x2.astype(F32)).astype(BF16)
    return x1, x2, x3


def _sigmoid(x):
    return 1.0 / (1.0 + jnp.exp(-x))


def _silu(x):
    return x * _sigmoid(x)


def _rms(x, w):
    return x * lax.rsqrt(jnp.mean(x * x, axis=-1, keepdims=True) + EPS) * w


def _mods_kernel(c_ref, w_ref, b_ref, o_ref):
    s = _silu(c_ref[...])
    s1, s2, s3 = _split3(s)
    w1, w2, w3 = _split3(w_ref[...])
    acc = _dot(s1, w1) + (_dot(s1, w2) + _dot(s2, w1)) + (_dot(s1, w3) + _dot(s2, w2) + _dot(s3, w1))
    o_ref[...] = acc + b_ref[...]


def _mods(cond, w_ada, b_ada):
    rows, d = cond.shape
    n = w_ada.shape[1]
    tn = _pick(n, (1024, 512, 256, 128))
    return pl.pallas_call(
        _mods_kernel,
        out_shape=jax.ShapeDtypeStruct((rows, n), F32),
        grid=(n // tn,),
        in_specs=[pl.BlockSpec((rows, d), lambda j: (0, 0)),
                  pl.BlockSpec((d, tn), lambda j: (0, j)),
                  pl.BlockSpec((1, tn), lambda j: (0, j))],
        out_specs=pl.BlockSpec((rows, tn), lambda j: (0, j)),
        compiler_params=_params("arbitrary"),
        name="mods",
    )(cond, w_ada, b_ada.reshape(1, n))


def _mod_index(tile, n_ctx_tiles, tiles_per_lat, n_lat):
    t = jnp.maximum(tile - n_ctx_tiles, 0)
    return jnp.where(tile < n_ctx_tiles, n_lat, lax.div(t, tiles_per_lat))


def _norm1_kernel(xc_ref, xs_ref, pos_ref, mod_ref, w_ref, wh_ref, wl_ref, x0_ref, hh_ref, dt_ref, *, n_ctx_tiles):
    i = pl.program_id(0)
    x = jnp.where(i < n_ctx_tiles, xc_ref[...], xs_ref[...] + pos_ref[...])
    x0_ref[...] = x
    h = _rms(x, w_ref[...]) * (1.0 + mod_ref[0, 1:2, :]) + mod_ref[0, 0:1, :]
    hh = h.astype(BF16)
    hh_ref[...] = hh
    hl = (h - hh.astype(F32)).astype(BF16)
    wh = wh_ref[...]
    dt_ref[...] = _dot(hh, wh) + (_dot(hh, wl_ref[...]) + _dot(hl, wh))


def _norm1(xc, xs, pos, mods3, w, wdt_h, wdt_l, *, l_lat, n_lat):
    n_ctx, d = xc.shape
    n_tok = n_ctx + xs.shape[0]
    n_dt = wdt_h.shape[1]
    tm = 256
    nct = n_ctx // tm
    tpl = l_lat // tm
    return pl.pallas_call(
        functools.partial(_norm1_kernel, n_ctx_tiles=nct),
        out_shape=(jax.ShapeDtypeStruct((n_tok, d), F32),
                   jax.ShapeDtypeStruct((n_tok, d), BF16),
                   jax.ShapeDtypeStruct((n_tok, n_dt), F32)),
        grid=(n_tok // tm,),
        in_specs=[pl.BlockSpec((tm, d), lambda i: (jnp.minimum(i, nct - 1), 0)),
                  pl.BlockSpec((tm, d), lambda i: (jnp.maximum(i - nct, 0), 0)),
                  pl.BlockSpec((tm, d), lambda i: (lax.rem(jnp.maximum(i - nct, 0), tpl), 0)),
                  pl.BlockSpec((1, 6, d), lambda i: (_mod_index(i, nct, tpl, n_lat), 0, 0)),
                  pl.BlockSpec((1, d), lambda i: (0, 0)),
                  pl.BlockSpec((d, n_dt), lambda i: (0, 0)),
                  pl.BlockSpec((d, n_dt), lambda i: (0, 0))],
        out_specs=(pl.BlockSpec((tm, d), lambda i: (i, 0)),
                   pl.BlockSpec((tm, d), lambda i: (i, 0)),
                   pl.BlockSpec((tm, n_dt), lambda i: (i, 0))),
        compiler_params=_params("arbitrary"),
        name="norm1",
    )(xc, xs, pos, mods3, w.reshape(1, d), wdt_h, wdt_l)


def _mm_kernel(a_ref, b_ref, o_ref):
    o_ref[...] = _dot(a_ref[...], b_ref[...]).astype(o_ref.dtype)


def _mm(a, b, out_dtype):
    m, k = a.shape
    n = b.shape[1]
    tm = _pick(m, (1024, 512, 256))
    tn = _pick(n, (1024, 512, 256, 128))
    return pl.pallas_call(
        _mm_kernel,
        out_shape=jax.ShapeDtypeStruct((m, n), out_dtype),
        grid=(m // tm, n // tn),
        in_specs=[pl.BlockSpec((tm, k), lambda i, j: (i, 0)),
                  pl.BlockSpec((k, tn), lambda i, j: (0, j))],
        out_specs=pl.BlockSpec((tm, tn), lambda i, j: (i, j)),
        compiler_params=_params("arbitrary", "arbitrary"),
        name="mm",
    )(a, b)


def _transpose_cast_kernel(x_ref, o_ref):
    o_ref[...] = x_ref[...].T.astype(o_ref.dtype)


def _transpose_cast(x, dtype):
    m, n = x.shape
    tm = _pick(m, (512, 256, 128))
    return pl.pallas_call(
        _transpose_cast_kernel,
        out_shape=jax.ShapeDtypeStruct((n, m), dtype),
        grid=(m // tm,),
        in_specs=[pl.BlockSpec((tm, n), lambda i: (i, 0))],
        out_specs=pl.BlockSpec((n, tm), lambda i: (0, i)),
        compiler_params=_params("arbitrary"),
        name="transpose_cast",
    )(x)


def _dft_mats(n):
    idx = np.arange(n, dtype=np.int64)
    ang = 2.0 * np.pi * ((idx[:, None] * idx[None, :]) % n).astype(np.float64) / n
    scale = 1.0 / math.sqrt(n)
    return np.cos(ang) * scale, np.sin(ang) * scale


def _span_dft(span, seq_len):
    c, s = _dft_mats(seq_len)
    eye = np.eye(span // seq_len)
    return np.kron(eye, c), np.kron(eye, s)


def _fourier_kernel(u_ref, cs_ref, cl_ref, sl_ref, wf_ref, o_ref, csb_ref, clb_ref, slb_ref, *, n_ctx_spans):
    s = pl.program_id(0)

    @pl.when(jnp.logical_or(s == 0, s == n_ctx_spans))
    def _():
        csb_ref[...] = cs_ref[...].astype(BF16)
        clb_ref[...] = cl_ref[0].astype(BF16)
        slb_ref[...] = sl_ref[0].astype(BF16)

    u = u_ref[...]
    pc, ps = [], []
    for g in range(FOURIER_GROUPS):
        p = _dot(u[:, g * FOURIER_GROUP_DIM:(g + 1) * FOURIER_GROUP_DIM], csb_ref[...])
        pc.append(p[:, :FOURIER_GROUP_DIM])
        ps.append(p[:, FOURIER_GROUP_DIM:])
    pc = jnp.concatenate(pc, axis=1).astype(BF16)
    ps = jnp.concatenate(ps, axis=1).astype(BF16)
    f = _dot(clb_ref[...], pc) - _dot(slb_ref[...], ps)
    o_ref[...] = _dot(f.astype(BF16), wf_ref[...]).astype(o_ref.dtype)


def _fourier(proj, w_fourier, *, span, l_ctx, n_ctx_spans, col_block):
    n_tok = proj.shape[0]
    fd = FOURIER_GROUPS * FOURIER_GROUP_DIM
    d = w_fourier.shape[1]
    cc, sc = _dft_mats(FOURIER_GROUP_DIM)
    cs = np.concatenate([cc, sc], axis=1)
    c_ctx, s_ctx = _span_dft(span, l_ctx)
    c_lat, s_lat = _dft_mats(span)
    cl = jnp.asarray(np.stack([c_ctx, c_lat]), F32)
    sl = jnp.asarray(np.stack([s_ctx, s_lat]), F32)
    kind = lambda s: (jnp.where(s < n_ctx_spans, 0, 1), 0, 0)
    return pl.pallas_call(
        functools.partial(_fourier_kernel, n_ctx_spans=n_ctx_spans),
        out_shape=jax.ShapeDtypeStruct((n_tok, d), BF16),
        grid=(n_tok // span,),
        in_specs=[pl.BlockSpec((span, fd), lambda s: (s, col_block)),
                  pl.BlockSpec(cs.shape, lambda s: (0, 0)),
                  pl.BlockSpec((1, span, span), kind),
                  pl.BlockSpec((1, span, span), kind),
                  pl.BlockSpec((fd, d), lambda s: (0, 0))],
        out_specs=pl.BlockSpec((span, d), lambda s: (s, 0)),
        scratch_shapes=[pltpu.VMEM(cs.shape, BF16), pltpu.VMEM((span, span), BF16), pltpu.VMEM((span, span), BF16)],
        compiler_params=_params("arbitrary"),
        name="fourier",
    )(proj, jnp.asarray(cs, F32), cl, sl, w_fourier)


def _conv_kernel(u_ref, w_ref, b_ref, o_ref, *, n_ctx_spans, l_ctx):
    n = u_ref.shape[0]
    half = CONV_W // 2
    seq_len = jnp.where(pl.program_id(0) < n_ctx_spans, l_ctx, n)

    def taps(u, pos=None):
        rows = u.shape[0]
        acc = b_ref[...] + u * w_ref[half:half + 1, :]
        for k in range(CONV_W):
            off = k - half
            if off == 0:
                continue
            shifted = pltpu.roll(u, (-off) % rows, axis=0)
            if pos is not None:
                shifted = jnp.where(jnp.logical_and(pos + off >= 0, pos + off < seq_len), shifted, 0.0)
            acc = acc + shifted * w_ref[k:k + 1, :]
        return _silu(acc).astype(o_ref.dtype)

    o_ref[...] = taps(u_ref[...].astype(F32))
    slab, fix = 4 * SUBLANES, 2 * SUBLANES
    for b in range(0, n + 1, l_ctx):
        lo = min(max(b - slab // 2, 0), n - slab)
        w0 = max(b - fix // 2, 0)
        w1 = min(b + fix // 2, n)
        pos = jnp.bitwise_and(lo + lax.broadcasted_iota(jnp.int32, (slab, u_ref.shape[1]), 0), seq_len - 1)
        redo = taps(u_ref[lo:lo + slab, :].astype(F32), pos)
        o_ref[w0:w1, :] = redo[w0 - lo:w1 - lo, :]


def _conv(proj, w_conv, b_conv, *, span, l_ctx, n_ctx_spans, col0):
    n_tok = proj.shape[0]
    cdim = w_conv.shape[1]
    tc = 512
    cb0 = col0 // tc
    return pl.pallas_call(
        functools.partial(_conv_kernel, n_ctx_spans=n_ctx_spans, l_ctx=l_ctx),
        out_shape=jax.ShapeDtypeStruct((n_tok, cdim), BF16),
        grid=(n_tok // span, cdim // tc),
        in_specs=[pl.BlockSpec((span, tc), lambda s, j: (s, cb0 + j)),
                  pl.BlockSpec((CONV_W, tc), lambda s, j: (0, j)),
                  pl.BlockSpec((1, tc), lambda s, j: (0, j))],
        out_specs=pl.BlockSpec((span, tc), lambda s, j: (s, j)),
        compiler_params=_params("arbitrary", "arbitrary"),
        name="conv",
    )(proj, w_conv, b_conv.reshape(1, cdim))


PAIRS_PER_GROUP = HEADS_PER_GROUP // 2


def _ssd_kernel(xf_ref, bf_ref, cf_ref, dtf_ref, xb_ref, bb_ref, cb_ref, dtb_ref, par_ref, dsk_ref, h0_ref,
                yf_ref, yb_ref, hf_ref, st_ref, *, n_ctx_spans, chunks_per_ctx_seq, n_chunks):
    s = pl.program_id(0)
    c = pl.program_id(2)
    is_ctx = s < n_ctx_spans
    cps = chunks_per_ctx_seq
    seq_start = jnp.where(is_ctx, lax.rem(c, cps) == 0, c == 0)
    seq_end = jnp.logical_and(is_ctx, lax.rem(c, cps) == cps - 1)

    @pl.when(jnp.logical_and(seq_start, is_ctx))
    def _():
        st_ref[...] = jnp.zeros(st_ref.shape, F32)

    @pl.when(jnp.logical_and(seq_start, jnp.logical_not(is_ctx)))
    def _():
        st_ref[...] = h0_ref[0]

    li = lax.broadcasted_iota(jnp.int32, (CHUNK, CHUNK), 0)
    si = lax.broadcasted_iota(jnp.int32, (CHUNK, CHUNK), 1)
    lane_lo = si < HEAD_DIM
    row_lo = li < HEAD_DIM
    gw = HEADS_PER_GROUP * HEAD_DIM
    n_groups = N_GROUPS

    for d, (x_ref, b_ref, c_ref, dt_ref, y_ref) in enumerate(((xf_ref, bf_ref, cf_ref, dtf_ref, yf_ref),
                                                                 (xb_ref, bb_ref, cb_ref, dtb_ref, yb_ref))):
        mask = (si <= li) if d == 0 else (si >= li)
        mb = mask.astype(BF16)
        raw = dt_ref[...] + par_ref[1:2, :]
        dt = jnp.maximum(raw, 0.0) + jnp.log1p(jnp.exp(-jnp.abs(raw)))
        da = dt * (-jnp.exp(par_ref[0:1, :]))
        d1, d2, d3 = _split3(da)
        cs = _dot(mb, d1) + _dot(mb, d2) + _dot(mb, d3)
        tot = cs[CHUNK - 1:CHUNK, :] if d == 0 else cs[0:1, :]
        w_t = (dt * jnp.exp(tot - cs)).T
        cs_t = cs.T
        dt_t = dt.T
        ecs = jnp.exp(cs)
        etot_t = jnp.exp(cs_t[:, CHUNK - 1:CHUNK] if d == 0 else cs_t[:, 0:1])
        for gi in range(n_groups):
            gl = slice(gi * LANES, (gi + 1) * LANES)
            bm = b_ref[:, gl]
            cm = c_ref[:, gl]
            cb = _dot_nt(cm, bm)
            x = x_ref[:, gi * gw:(gi + 1) * gw]
            x_t = x.astype(F32).T
            for kp in range(PAIRS_PER_GROUP):
                l0 = (d * n_groups + gi) * HEADS_PER_GROUP + 2 * kp
                l1 = l0 + 1
                ps = slice(kp * LANES, (kp + 1) * LANES)
                ys = slice(gi * gw + kp * LANES, gi * gw + (kp + 1) * LANES)
                pair = gi * PAIRS_PER_GROUP + kp
                xp = x[:, ps]
                lms = []
                for ln in (l0, l1):
                    seg = cs[:, ln:ln + 1] - cs_t[ln:ln + 1, :]
                    dec = jnp.exp(jnp.where(mask, seg, -jnp.inf))
                    lms.append((cb * dec * dt_t[ln:ln + 1, :]).astype(BF16))
                hp = st_ref[d, pair]
                y = jnp.where(lane_lo, _dot(lms[0], xp), _dot(lms[1], xp))
                y = y + _dot_nt(cm, hp.astype(BF16)) * jnp.where(lane_lo, ecs[:, l0:l0 + 1], ecs[:, l1:l1 + 1])
                y = y + dsk_ref[d:d + 1, ys] * xp.astype(F32)
                y_ref[:, ys] = y.astype(y_ref.dtype)
                xw = jnp.where(row_lo, w_t[l0:l0 + 1, :], w_t[l1:l1 + 1, :]) * x_t[ps, :]
                keep = jnp.where(row_lo, etot_t[l0:l0 + 1, :], etot_t[l1:l1 + 1, :])
                st_ref[d, pair] = hp * keep + _dot(xw.astype(BF16), bm)

    @pl.when(seq_end)
    def _():
        hf_ref[lax.div(c, cps), 0] = st_ref[0]
        hf_ref[lax.div(n_chunks - 1 - c, cps), 1] = st_ref[1]


def _ssd(xbc, dt, par, dsk, h0, *, span, l_ctx, n_ctx_spans, n_ctx_seq):
    n_tok = xbc.shape[0]
    d_inner = N_GROUPS * HEADS_PER_GROUP * HEAD_DIM
    nc = span // CHUNK
    cps = l_ctx // CHUNK
    spc = span // l_ctx
    gps = N_GROUPS
    n_gsteps = N_GROUPS // gps
    gw = HEADS_PER_GROUP * HEAD_DIM * gps
    sw = D_STATE * gps
    b_cb = d_inner // sw
    c_cb = b_cb + n_gsteps
    st_block = (1, PAIRS_PER_GROUP * gps, 2 * HEAD_DIM, D_STATE)

    rf = lambda s, g, c: s * nc + c
    rb = lambda s, g, c: s * nc + (nc - 1 - c)

    hf_block = (spc, 2) + st_block[1:]
    hf_index = lambda s, g, c: (jnp.minimum(s, n_ctx_spans - 1), 0, 0, 0, 0)

    in_specs = []
    for r in (rf, rb):
        in_specs += [pl.BlockSpec((CHUNK, gw), lambda s, g, c, r=r: (r(s, g, c), g)),
                     pl.BlockSpec((CHUNK, sw), lambda s, g, c, r=r: (r(s, g, c), b_cb + g)),
                     pl.BlockSpec((CHUNK, sw), lambda s, g, c, r=r: (r(s, g, c), c_cb + g)),
                     pl.BlockSpec((CHUNK, LANES), lambda s, g, c, r=r: (r(s, g, c), 0))]
    in_specs += [pl.BlockSpec((8, LANES), lambda s, g, c: (0, 0)),
                 pl.BlockSpec((2, gw), lambda s, g, c: (0, g)),
                 pl.BlockSpec((1, 2) + st_block[1:], lambda s, g, c: (jnp.maximum(s - n_ctx_spans, 0), 0, g, 0, 0))]
    hf_shape = jax.ShapeDtypeStruct((n_ctx_seq, 2, N_GROUPS * PAIRS_PER_GROUP, 2 * HEAD_DIM, D_STATE), F32)
    return pl.pallas_call(
        functools.partial(_ssd_kernel, n_ctx_spans=n_ctx_spans, chunks_per_ctx_seq=cps, n_chunks=nc),
        out_shape=(jax.ShapeDtypeStruct((n_tok, d_inner), BF16), jax.ShapeDtypeStruct((n_tok, d_inner), BF16),
                   hf_shape),
        grid=(n_tok // span, n_gsteps, nc),
        in_specs=in_specs,
        out_specs=(pl.BlockSpec((CHUNK, gw), lambda s, g, c: (rf(s, g, c), g)),
                   pl.BlockSpec((CHUNK, gw), lambda s, g, c: (rb(s, g, c), g)),
                   pl.BlockSpec(hf_block, hf_index)),
        scratch_shapes=[pltpu.VMEM((2,) + st_block[1:], F32)],
        compiler_params=_params("arbitrary", "arbitrary", "arbitrary"),
        name="ssd",
    )(xbc, xbc, xbc, dt, xbc, xbc, xbc, dt, par, dsk, h0)


def _merge_kernel(yf_ref, yb_ref, z_ref, g_ref, a_ref, x0_ref, mod_ref, nssd_ref, wso_ref, wo_ref,
                  npost_ref, nffn_ref, x1_ref, h2_ref):
    d = x0_ref.shape[1]
    z = z_ref[...].astype(F32)
    y = (yf_ref[...].astype(F32) + yb_ref[...].astype(F32)) * _silu(z)
    y = _rms(y, nssd_ref[...])
    b_out = _dot(y.astype(BF16), wso_ref[...])
    g = g_ref[...].astype(F32)
    merged = _sigmoid(g[:, :d]) * a_ref[...].astype(F32) + _sigmoid(g[:, d:]) * b_out
    mix = _dot(merged.astype(BF16), wo_ref[...])
    x1 = x0_ref[...] + mod_ref[0, 2:3, :] * _rms(mix, npost_ref[...])
    x1_ref[...] = x1
    h2 = _rms(x1, nffn_ref[...]) * (1.0 + mod_ref[0, 4:5, :]) + mod_ref[0, 3:4, :]
    h2_ref[...] = h2.astype(BF16)


def _merge(yf, yb, proj, a_out, x0, mods3, ssd_norm, wso, wo, npost, nffn, *, z_cb, g_cb, n_ctx, l_lat, n_lat):
    n_tok, d = x0.shape
    d_inner = yf.shape[1]
    tm = 256
    nct, tpl = n_ctx // tm, l_lat // tm
    row = lambda i: (i, 0)
    const = lambda i: (0, 0)
    return pl.pallas_call(
        _merge_kernel,
        out_shape=(jax.ShapeDtypeStruct((n_tok, d), F32), jax.ShapeDtypeStruct((n_tok, d), BF16)),
        grid=(n_tok // tm,),
        in_specs=[pl.BlockSpec((tm, d_inner), row),
                  pl.BlockSpec((tm, d_inner), row),
                  pl.BlockSpec((tm, d_inner), lambda i: (i, z_cb)),
                  pl.BlockSpec((tm, 2 * d), lambda i: (i, g_cb)),
                  pl.BlockSpec((tm, d), row),
                  pl.BlockSpec((tm, d), row),
                  pl.BlockSpec((1, 6, d), lambda i: (_mod_index(i, nct, tpl, n_lat), 0, 0)),
                  pl.BlockSpec((1, d_inner), const),
                  pl.BlockSpec((d_inner, d), const),
                  pl.BlockSpec((d, d), const),
                  pl.BlockSpec((1, d), const),
                  pl.BlockSpec((1, d), const)],
        out_specs=(pl.BlockSpec((tm, d), row), pl.BlockSpec((tm, d), row)),
        compiler_params=_params("arbitrary"),
        name="merge",
    )(yf, yb, proj, proj, a_out, x0, mods3, ssd_norm.reshape(1, d_inner), wso, wo,
      npost.reshape(1, d), nffn.reshape(1, d))


SUBLANES = 8
ROUTER_TOKEN_BLOCK = SUBLANES * LANES


def _sort_pairs(n):
    pairs = []
    p = 1
    while p < n:
        k = p
        while k >= 1:
            for j in range(k % p, n - k, 2 * k):
                for i in range(min(k, n - j - k)):
                    if (i + j) // (2 * p) == (i + j + k) // (2 * p):
                        pairs.append((i + j, i + j + k))
            k //= 2
        p *= 2
    return pairs


def _sort_desc(v):
    v = list(v)
    for a, b in _sort_pairs(len(v)):
        v[a], v[b] = jnp.maximum(v[a], v[b]), jnp.minimum(v[a], v[b])
    return v


def _merge_top(a, b):
    n = len(a)
    c = [jnp.maximum(a[i], b[n - 1 - i]) for i in range(n)]
    s = n // 2
    while s >= 1:
        for i in range(n):
            if i & s == 0:
                c[i], c[i + s] = jnp.maximum(c[i], c[i + s]), jnp.minimum(c[i], c[i + s])
        s //= 2
    return c


def _top_sorted(items, keep=TOPK):
    groups = [_sort_desc(items[g:g + keep]) for g in range(0, len(items), keep)]
    while len(groups) > 1:
        groups = [_merge_top(groups[i], groups[i + 1]) for i in range(0, len(groups), 2)]
    return groups[0]


_STAIRCASE = [(k, l) for k in range(TOPK) for l in range(TOPK) if (k + 1) * (l + 1) <= TOPK]
_STAIRCASE_PAD = 4 * TOPK - len(_STAIRCASE)


def _router_kernel(h2_ref, wq_ref, keys_ref, e1_ref, e2_ref, th_ref, s_ref, k8_ref):
    def to_row(v8):
        return jnp.concatenate([v8[r:r + 1, :] for r in range(SUBLANES)], axis=1)

    def head(h, carry):
        q = _dot(h2_ref[...], wq_ref[h])
        tops, maxes = [], []
        for half in range(2):
            s_t = _dot_nt(keys_ref[h, half], q[:, half * N_KEYS:(half + 1) * N_KEYS].astype(BF16))
            s_ref[half] = s_t
            for r in range(SUBLANES):
                k8_ref[half, pl.ds(r, N_KEYS, stride=SUBLANES), :] = s_t[:, r * LANES:(r + 1) * LANES]
            top = _top_sorted([k8_ref[half, k * SUBLANES:(k + 1) * SUBLANES, :] for k in range(N_KEYS)])
            maxes.append(top[0])
            tops.append([jnp.exp(t - top[0]) for t in top])
        a, b = tops
        cand = [a[k] * b[l] for k, l in _STAIRCASE]
        best = _top_sorted(cand + [jnp.full_like(cand[0], -1.0)] * _STAIRCASE_PAD)
        theta = best[-1]
        inv_z = 1.0 / functools.reduce(lambda u, v: u + v, best)
        a_n = [x * inv_z for x in a]
        th = None
        for (k, l), c in zip(_STAIRCASE, cand):
            v = jnp.where(c >= theta, b[l] * a_n[k], jnp.inf)
            th = v if th is None else jnp.minimum(th, v)
        e1_ref[h] = jnp.exp(s_ref[0] - to_row(maxes[0])) * to_row(inv_z)
        e2_ref[h] = jnp.exp(s_ref[1] - to_row(maxes[1]))
        th_ref[pl.ds(h, 1), :] = to_row(th)
        return carry

    lax.fori_loop(0, PEER_HEADS, head, 0)


def _router(h2, wq, keys):
    n_tok, d = h2.shape
    tb = ROUTER_TOKEN_BLOCK
    return pl.pallas_call(
        _router_kernel,
        out_shape=(jax.ShapeDtypeStruct((PEER_HEADS, N_KEYS, n_tok), F32),
                   jax.ShapeDtypeStruct((PEER_HEADS, N_KEYS, n_tok), F32),
                   jax.ShapeDtypeStruct((PEER_HEADS, n_tok), F32)),
        grid=(n_tok // tb,),
        in_specs=[pl.BlockSpec((tb, d), lambda i: (i, 0)),
                  pl.BlockSpec(wq.shape, lambda i: (0, 0, 0)),
                  pl.BlockSpec(keys.shape, lambda i: (0, 0, 0, 0))],
        out_specs=(pl.BlockSpec((PEER_HEADS, N_KEYS, tb), lambda i: (0, 0, i)),
                   pl.BlockSpec((PEER_HEADS, N_KEYS, tb), lambda i: (0, 0, i)),
                   pl.BlockSpec((PEER_HEADS, tb), lambda i: (0, i))),
        scratch_shapes=[pltpu.VMEM((2, N_KEYS, tb), F32), pltpu.VMEM((2, N_KEYS * SUBLANES, LANES), F32)],
        compiler_params=_params("arbitrary"),
        name="router",
    )(h2, wq, keys)


PEER_TOKEN_BLOCK = 512
PEER_EXPERT_CHUNK = 512
PEER_CHUNKS_PER_STEP = 4


GELU_C = math.sqrt(2.0 / math.pi)
GELU_K = 0.044715


def _gelu_tanh(x, c, k):
    return 0.5 * x * (1.0 + jnp.tanh(c * (x + k * (x * x * x))))


def _peer_weighted_act(e1_ref, i0, e2_ref, th_ref, gc_ref, at_ref, gw_ref):
    e_chunk, tb = at_ref.shape
    gc = gc_ref[0:1, :].astype(at_ref.dtype)
    gk = gc_ref[1:2, :].astype(at_ref.dtype)
    for ii in range(e_chunk // N_KEYS):
        rs = slice(ii * N_KEYS, (ii + 1) * N_KEYS)
        for lb in range(tb // LANES):
            ls = slice(lb * LANES, (lb + 1) * LANES)
            w = None
            for h in range(PEER_HEADS):
                p = e2_ref[h, :, ls] * e1_ref[h, i0 + ii:i0 + ii + 1, ls]
                sel = jnp.where(p >= th_ref[h:h + 1, ls], p, 0.0)
                w = sel if w is None else w + sel
            gw_ref[rs, ls] = w.astype(BF16) * _gelu_tanh(at_ref[rs, ls], gc, gk)


def _peer_kernel(h2_ref, u_ref, vt_ref, e1_ref, e2_ref, th_ref, gc_ref, x1_ref, mod_ref, npost_ref,
                 yc_ref, ys_ref, acc_ref, *bufs, n_it, n_ctx_blocks):
    it = pl.program_id(1)
    n_sub = PEER_CHUNKS_PER_STEP
    at_refs, gw_refs = bufs[:n_sub], bufs[n_sub:]
    ec = at_refs[0].shape[0]

    @pl.when(it == 0)
    def _():
        acc_ref[...] = jnp.zeros(acc_ref.shape, F32)

    h2 = h2_ref[...]

    def project(k):
        at_refs[k][...] = _dot_nt(u_ref[k * ec:(k + 1) * ec, :], h2).astype(BF16)

    def weigh(k):
        _peer_weighted_act(e1_ref, k * (ec // N_KEYS), e2_ref, th_ref, gc_ref, at_refs[k], gw_refs[k])

    def combine(k):
        acc_ref[...] += _dot(vt_ref[:, k * ec:(k + 1) * ec], gw_refs[k][...])

    project(0)
    for k in range(n_sub):
        if k + 1 < n_sub:
            project(k + 1)
        weigh(k)
        combine(k)

    last = it == n_it - 1
    is_ctx = pl.program_id(0) < n_ctx_blocks

    def result():
        o = acc_ref[...].T
        return x1_ref[...] + mod_ref[0, 5:6, :] * _rms(o, npost_ref[...])

    @pl.when(jnp.logical_and(last, is_ctx))
    def _():
        yc_ref[...] = result()

    @pl.when(jnp.logical_and(last, jnp.logical_not(is_ctx)))
    def _():
        ys_ref[...] = result()


def _peer(h2, u, vt, e1, e2, th, x1, mods3, npost, *, n_ctx, l_lat, n_lat):
    n_tok, d = h2.shape
    n_exp = u.shape[0]
    tb, ec, n_sub = PEER_TOKEN_BLOCK, PEER_EXPERT_CHUNK, PEER_CHUNKS_PER_STEP
    step_e = ec * n_sub
    n_it = n_exp // step_e
    rows_e1 = step_e // N_KEYS
    nct, tpl = n_ctx // tb, l_lat // tb
    gelu_consts = jnp.asarray(np.repeat(np.array([[GELU_C], [GELU_K]] + [[0.0]] * 6, np.float32), LANES, axis=1))
    return pl.pallas_call(
        functools.partial(_peer_kernel, n_it=n_it, n_ctx_blocks=nct),
        out_shape=(jax.ShapeDtypeStruct((n_ctx, d), F32), jax.ShapeDtypeStruct((n_tok - n_ctx, d), F32)),
        grid=(n_tok // tb, n_it),
        in_specs=[pl.BlockSpec((tb, d), lambda i, t: (i, 0)),
                  pl.BlockSpec((step_e, d), lambda i, t: (t, 0)),
                  pl.BlockSpec((d, step_e), lambda i, t: (0, t)),
                  pl.BlockSpec((PEER_HEADS, rows_e1, tb), lambda i, t: (0, t, i)),
                  pl.BlockSpec((PEER_HEADS, N_KEYS, tb), lambda i, t: (0, 0, i)),
                  pl.BlockSpec((PEER_HEADS, tb), lambda i, t: (0, i)),
                  pl.BlockSpec(gelu_consts.shape, lambda i, t: (0, 0)),
                  pl.BlockSpec((tb, d), lambda i, t: (i, 0)),
                  pl.BlockSpec((1, 6, d), lambda i, t: (_mod_index(i, nct, tpl, n_lat), 0, 0)),
                  pl.BlockSpec((1, d), lambda i, t: (0, 0))],
        out_specs=(pl.BlockSpec((tb, d), lambda i, t: (jnp.minimum(i, nct - 1), 0)),
                   pl.BlockSpec((tb, d), lambda i, t: (jnp.maximum(i - nct, 0), 0))),
        scratch_shapes=[pltpu.VMEM((d, tb), F32)] + [pltpu.VMEM((ec, tb), BF16) for _ in range(2 * n_sub)],
        compiler_params=_params("arbitrary", "arbitrary"),
        name="peer",
    )(h2, u, vt, e1, e2, th, gelu_consts, x1, mods3, npost.reshape(1, d))


def _grid_posemb(rows, d):
    r = np.repeat(np.arange(rows, dtype=np.float32), GRID_W)
    col = np.tile(np.arange(GRID_W, dtype=np.float32), rows)
    quarter = d // 4
    omega = (1.0 / (10000.0 ** (np.arange(quarter, dtype=np.float32) / np.float32(quarter)))).astype(np.float32)
    ar = r[:, None] * omega
    ac = col[:, None] * omega
    return np.concatenate([np.sin(ar), np.cos(ar), np.sin(ac), np.cos(ac)], axis=-1).astype(np.float32)


def kernel(x_prompt, x_sample, state_ssd, c, c_ctx, w_ada, b_ada, norm_mix_pre, norm_mix_post, norm_ffn_pre,
           norm_ffn_post, w_in, w_conv, b_conv, a_log, dt_bias, d_skip, ssd_norm, w_fourier, w_ssd_out, w_out,
           peer_query, peer_sub_keys, peer_u, peer_v):
    n_ctx_seq, l_ctx, d = x_prompt.shape
    n_lat, l_lat, _ = x_sample.shape
    depth = w_ada.shape[0]
    assert depth == 1
    n_ctx = n_ctx_seq * l_ctx
    span = l_lat
    assert span % l_ctx == 0 and n_ctx % span == 0 and n_ctx > 0 and n_lat > 0
    assert l_ctx & (l_ctx - 1) == 0 and span & (span - 1) == 0
    assert l_ctx % 256 == 0 and span % PEER_TOKEN_BLOCK == 0 and n_ctx % PEER_TOKEN_BLOCK == 0
    n_ctx_spans = n_ctx // span
    fd = FOURIER_GROUPS * FOURIER_GROUP_DIM
    d_inner = 2 * d
    n_heads = d_inner // HEAD_DIM
    assert n_heads == N_GROUPS * HEADS_PER_GROUP
    gn = N_GROUPS * D_STATE
    conv_dim = d_inner + 2 * gn
    col_z, col_xbc, col_dt = fd, fd + d_inner, fd + d_inner + conv_dim
    col_gate = col_dt + 2 * n_heads
    l = 0

    rows = n_lat + 1
    rows_pad = -(-rows // 8) * 8
    cond = jnp.concatenate([c, c_ctx[None, :], jnp.zeros((rows_pad - rows, d), F32)], axis=0)
    mods3 = _mods(cond, w_ada[l], b_ada[l]).reshape(rows_pad, 6, d)

    wi = w_in[l]
    assert 2 * n_heads <= LANES
    wdt = jnp.pad(wi[:, col_dt:col_gate], ((0, 0), (0, LANES - 2 * n_heads)))
    wdt_h = wdt.astype(BF16)
    wdt_l = (wdt - wdt_h.astype(F32)).astype(BF16)

    pos = jnp.asarray(_grid_posemb(l_lat // GRID_W, d))
    x0, hh, dt_raw = _norm1(x_prompt.reshape(n_ctx, d), x_sample.reshape(n_lat * l_lat, d), pos, mods3,
                            norm_mix_pre[l], wdt_h, wdt_l, l_lat=l_lat, n_lat=n_lat)

    w_main = jnp.concatenate([wi[:, col_z:col_xbc], wi[:, col_gate:], wi[:, :fd], wi[:, col_xbc:col_dt]],
                             axis=1).astype(BF16)
    proj = _mm(hh, w_main, BF16)
    z_cb, g_cb = 0, 1
    f_cb = (2 * d_inner) // fd
    xbc_col0 = 2 * d_inner + fd

    lanes = lambda p: jnp.pad(p.reshape(1, 2 * n_heads), ((0, 0), (0, LANES - 2 * n_heads)))
    par = jnp.concatenate([lanes(a_log[l]), lanes(dt_bias[l]), jnp.zeros((6, LANES), F32)], axis=0)
    dsk = jnp.repeat(d_skip[l], HEAD_DIM, axis=1)

    a_out = _fourier(proj, w_fourier[l].astype(BF16), span=span, l_ctx=l_ctx, n_ctx_spans=n_ctx_spans, col_block=f_cb)
    xbc = _conv(proj, w_conv[l], b_conv[l], span=span, l_ctx=l_ctx, n_ctx_spans=n_ctx_spans, col0=xbc_col0)
    h0 = state_ssd[:, l].reshape(n_lat, 2, N_GROUPS * PAIRS_PER_GROUP, 2 * HEAD_DIM, D_STATE)
    yf, yb, hfin = _ssd(xbc, dt_raw, par, dsk, h0, span=span, l_ctx=l_ctx, n_ctx_spans=n_ctx_spans,
                        n_ctx_seq=n_ctx_seq)

    x1, h2 = _merge(yf, yb, proj, a_out, x0, mods3, ssd_norm[l], w_ssd_out[l].astype(BF16), w_out[l].astype(BF16),
                    norm_mix_post[l], norm_ffn_pre[l], z_cb=z_cb, g_cb=g_cb, n_ctx=n_ctx, l_lat=l_lat, n_lat=n_lat)

    wq = jnp.transpose(peer_query[l].reshape(d, PEER_HEADS, 2 * N_KEYS), (1, 0, 2)).astype(BF16)
    e1, e2, th = _router(h2, wq, peer_sub_keys[l].astype(BF16))
    yc, ys = _peer(h2, peer_u[l].astype(BF16), _transpose_cast(peer_v[l], BF16), e1, e2, th, x1, mods3, norm_ffn_post[l],
                   n_ctx=n_ctx, l_lat=l_lat, n_lat=n_lat)

    new_state = hfin.reshape(n_ctx_seq, 1, 2, n_heads, HEAD_DIM, D_STATE).astype(x_prompt.dtype)
    return (yc.reshape(n_ctx_seq, l_ctx, d), ys.reshape(n_lat, l_lat, d), new_state)
```
